```python
import jax, jax.numpy as jnp
from jax import lax
import numpy as np

D_MODEL = 1024
BATCH = 8
SEQ = 4096
DEPTH = 2

HEAD_DIM = 64
N_HEADS_A = 6
N_HEADS_C = 6
WIDTH_A = N_HEADS_A * HEAD_DIM
WIDTH_C = N_HEADS_C * HEAD_DIM
WIDTH_B = D_MODEL - WIDTH_A - WIDTH_C
POOL_WINDOWS = (2, 4, 8, 16)
N_POOL_GROUPS = len(POOL_WINDOWS)
POOL_GROUP_DIM = WIDTH_B // N_POOL_GROUPS
DILATED_CONFIGS = ((128, 1), (512, 4), (2048, 16))
GRID_W = 64
NA_ROWS_MAX = 8
NA_COLS = 16
D_FF = -(-(8 * D_MODEL) // (3 * 256)) * 256
PROJ_WIDTH = 3 * WIDTH_A + WIDTH_B + 3 * WIDTH_C
EPS = 1e-6
NEG = -1e30

kernel_name = "hybrid_dilated_pool_neighbourhood_encoder"


def rmsnorm(x, g):
    xf = x.astype(jnp.float32)
    y = xf * lax.rsqrt(jnp.mean(xf * xf, axis=-1, keepdims=True) + EPS)
    return (y * g.astype(jnp.float32)).astype(x.dtype)


def to_heads(a, n_heads):
    b, s, _ = a.shape
    return a.reshape(b, s, n_heads, HEAD_DIM).transpose(0, 2, 1, 3)


def from_heads(a):
    b, h, s, d = a.shape
    return a.transpose(0, 2, 1, 3).reshape(b, s, h * d)


def dilated_window_branch(q, k, v, slopes, window, dil):
    B, H, S, hd = q.shape
    half = window // (2 * dil)
    blk = half
    chunk = dil * blk
    L = -(-S // chunk) * chunk
    n = L // dil
    nb = n // blk

    def prep(a):
        a = jnp.pad(a, ((0, 0), (0, 0), (0, L - S), (0, 0)))
        a = a.reshape(B, H, n, dil, hd).transpose(0, 1, 3, 2, 4)
        return a.reshape(B, H, dil, nb, blk, hd)

    def band(a):
        z = jnp.zeros_like(a[:, :, :, :1])
        prev = jnp.concatenate([z, a[:, :, :, :-1]], axis=3)
        nxt = jnp.concatenate([a[:, :, :, 1:], z], axis=3)
        return jnp.concatenate([prev, a, nxt], axis=4)

    qb = prep(q)
    kn = band(prep(k))
    vn = band(prep(v))

    qi = jnp.arange(nb)[:, None] * blk + jnp.arange(blk)[None, :]
    ki = (jnp.arange(nb)[:, None] - 1) * blk + jnp.arange(3 * blk)[None, :]
    rel = ki[:, None, :] - qi[:, :, None]
    kpos = ki[None] * dil + jnp.arange(dil)[:, None, None]
    kvalid = (ki >= 0)[None] & (kpos < S)
    allowed = (jnp.abs(rel) <= half)[None] & kvalid[:, :, None, :]
    dist = (jnp.abs(rel) * dil).astype(jnp.float32)

    s = jnp.einsum('bhrnqd,bhrnkd->bhrnqk', qb, kn,
                   preferred_element_type=jnp.float32) * (hd ** -0.5)
    s = s - slopes[None, :, None, None, None, None] * dist[None, None, None]
    s = jnp.where(allowed[None, None], s, NEG)
    m = jnp.max(s, axis=-1, keepdims=True)
    p = jnp.exp(s - m)
    den = jnp.sum(p, axis=-1)
    o = jnp.einsum('bhrnqk,bhrnkd->bhrnqd', p.astype(v.dtype), vn,
                   preferred_element_type=jnp.float32) / den[..., None]
    lse = m[..., 0] + jnp.log(den)
    o = o.reshape(B, H, dil, n, hd).transpose(0, 1, 3, 2, 4).reshape(B, H, L, hd)[:, :, :S]
    lse = lse.reshape(B, H, dil, n).transpose(0, 1, 3, 2).reshape(B, H, L)[:, :, :S]
    return o, lse


def dilated_mixture_attention(q, k, v):
    n_h = q.shape[1]
    slopes = 2.0 ** (-8.0 * (jnp.arange(n_h, dtype=jnp.float32) + 1.0) / n_h)
    outs, lses = [], []
    for window, dil in DILATED_CONFIGS:
        o, lse = dilated_window_branch(q, k, v, slopes, window, dil)
        outs.append(o)
        lses.append(lse)
    w = jax.nn.softmax(jnp.stack(lses, axis=0), axis=0)
    return jnp.sum(w[..., None] * jnp.stack(outs, axis=0), axis=0)


def neighbourhood_attention(q, k, v, rpb):
    B, H, S, hd = q.shape
    R = S // GRID_W
    wr = min(NA_ROWS_MAX, R)
    q = q.reshape(B, H, R, GRID_W, hd)
    k = k.reshape(B, H, R, GRID_W, hd)
    v = v.reshape(B, H, R, GRID_W, hd)
    rows = jnp.arange(R)
    rstart = jnp.clip(rows - wr // 2, 0, R - wr)
    krow = rstart[:, None] + jnp.arange(wr)[None, :]
    kg = k[:, :, krow]
    vg = v[:, :, krow]
    cols = jnp.arange(GRID_W)
    cstart = jnp.clip(cols - NA_COLS // 2, 0, GRID_W - NA_COLS)
    col_in = (cols[None, :] >= cstart[:, None]) & (cols[None, :] < cstart[:, None] + NA_COLS)
    s = jnp.einsum('bhrqd,bhrikd->bhrqik', q, kg,
                   preferred_element_type=jnp.float32) * (hd ** -0.5)
    dr = krow - rows[:, None] + (NA_ROWS_MAX - 1)
    dc = jnp.clip(cols[None, :] - cols[:, None] + NA_COLS - 1, 0, 2 * NA_COLS - 2)
    bias = rpb[:, dr[:, None, :, None], dc[None, :, None, :]]
    s = s + bias[None].astype(jnp.float32)
    s = jnp.where(col_in[:, None, :], s, NEG)
    p = jax.nn.softmax(s, axis=(-2, -1))
    o = jnp.einsum('bhrqik,bhrikd->bhrqd', p.astype(v.dtype), vg,
                   preferred_element_type=jnp.float32)
    return o.reshape(B, H, S, hd)


def multiscale_pool(u, w_pool, pool_scale):
    B, S, _ = u.shape
    uf = u.astype(jnp.float32)
    csum = jnp.concatenate([jnp.zeros((B, 1, WIDTH_B), jnp.float32), jnp.cumsum(uf, axis=1)], axis=1)
    t = jnp.arange(S)
    outs = []
    for g, w in enumerate(POOL_WINDOWS):
        lo = jnp.clip(t - w // 2, 0, S - 1)
        hi = jnp.clip(t + w // 2 - 1, 0, S - 1)
        seg = csum[:, :, g * POOL_GROUP_DIM:(g + 1) * POOL_GROUP_DIM]
        tot = seg[:, hi + 1] - seg[:, lo]
        cnt = (hi - lo + 1).astype(jnp.float32)
        outs.append(tot / cnt[None, :, None])
    pooled = jnp.concatenate(outs, axis=-1) - uf
    pooled = pooled.reshape(B, S, N_POOL_GROUPS, POOL_GROUP_DIM)
    y = jnp.einsum('bsgc,gcd->bsgd', pooled, w_pool.astype(jnp.float32)).reshape(B, S, WIDTH_B)
    return y * pool_scale.astype(jnp.float32)


def setup_inputs(seed: int = 0) -> dict:
    key = jax.random.key(seed)
    ks = jax.random.split(key, 16)
    f32 = jnp.float32
    nrm = lambda k, shape, sc: jax.random.normal(k, shape, f32) * sc
    return {
        "x": nrm(ks[0], (BATCH, SEQ, D_MODEL), 1.0),
        "c": nrm(ks[1], (BATCH, D_MODEL), 1.0),
        "w_ada": nrm(ks[2], (DEPTH, D_MODEL, 6 * D_MODEL), D_MODEL ** -0.5),
        "b_ada": nrm(ks[3], (DEPTH, 6 * D_MODEL), 0.02),
        "norm_mix": 1.0 + nrm(ks[4], (DEPTH, D_MODEL), 0.05),
        "w_in": nrm(ks[5], (DEPTH, D_MODEL, PROJ_WIDTH), D_MODEL ** -0.5),
        "norm_a_out": 1.0 + nrm(ks[6], (DEPTH, WIDTH_A), 0.05),
        "norm_c_out": 1.0 + nrm(ks[7], (DEPTH, WIDTH_C), 0.05),
        "w_pool": nrm(ks[8], (DEPTH, N_POOL_GROUPS, POOL_GROUP_DIM, POOL_GROUP_DIM), POOL_GROUP_DIM ** -0.5),
        "pool_scale": 1.0 + nrm(ks[9], (DEPTH, WIDTH_B), 0.1),
        "rpb": nrm(ks[10], (DEPTH, N_HEADS_C, 2 * NA_ROWS_MAX - 1, 2 * NA_COLS - 1), 0.1),
        "w_out": nrm(ks[11], (DEPTH, D_MODEL, D_MODEL), D_MODEL ** -0.5),
        "norm_ffn": 1.0 + nrm(ks[12], (DEPTH, D_MODEL), 0.05),
        "w_ffn_in": nrm(ks[13], (DEPTH, D_MODEL, 2 * D_FF), D_MODEL ** -0.5),
        "w_ffn_out": nrm(ks[14], (DEPTH, D_FF, D_MODEL), D_FF ** -0.5),
        "norm_final": 1.0 + nrm(ks[15], (D_MODEL,), 0.05),
    }


def reference(x, c, w_ada, b_ada, norm_mix, w_in, norm_a_out, norm_c_out, w_pool,
              pool_scale, rpb, w_out, norm_ffn, w_ffn_in, w_ffn_out, norm_final):
    dt = x.dtype
    c_act = jax.nn.silu(c)
    splits = np.cumsum([WIDTH_A, WIDTH_A, WIDTH_A, WIDTH_B, WIDTH_C, WIDTH_C]).tolist()
    for l in range(DEPTH):
        mod = c_act @ w_ada[l] + b_ada[l]
        sh1, sc1, g1, sh2, sc2, g2 = jnp.split(mod, 6, axis=-1)

        h = rmsnorm(x, norm_mix[l]) * (1.0 + sc1[:, None]) + sh1[:, None]
        z = h @ w_in[l]
        qa, ka, va, ub, qc, kc, vc = jnp.split(z, splits, axis=-1)

        oa = dilated_mixture_attention(to_heads(qa, N_HEADS_A), to_heads(ka, N_HEADS_A),
                                       to_heads(va, N_HEADS_A))
        oa = rmsnorm(from_heads(oa), norm_a_out[l]).astype(dt)
        ob = multiscale_pool(ub, w_pool[l], pool_scale[l]).astype(dt)
        oc = neighbourhood_attention(to_heads(qc, N_HEADS_C), to_heads(kc, N_HEADS_C),
                                     to_heads(vc, N_HEADS_C), rpb[l])
        oc = rmsnorm(from_heads(oc), norm_c_out[l]).astype(dt)

        mix = jnp.concatenate([oa, ob, oc], axis=-1) @ w_out[l]
        x = x + g1[:, None] * mix

        h2 = rmsnorm(x, norm_ffn[l]) * (1.0 + sc2[:, None]) + sh2[:, None]
        gate, up = jnp.split(h2 @ w_ffn_in[l], 2, axis=-1)
        ffn = (jax.nn.silu(gate) * up) @ w_ffn_out[l]
        x = x + g2[:, None] * ffn
    return rmsnorm(x, norm_final)
```

```python
import functools

import numpy as np
import jax
import jax.numpy as jnp
from jax import lax
from jax.experimental import pallas as pl
from jax.experimental.pallas import tpu as pltpu

F32 = jnp.float32
BF16 = jnp.bfloat16

HEAD_DIM = 64
N_HEADS_A = 6
N_HEADS_C = 6
WIDTH_A = N_HEADS_A * HEAD_DIM
WIDTH_C = N_HEADS_C * HEAD_DIM
POOL_WINDOWS = (2, 4, 8, 16)
POOL_GROUP_DIM = 64
WIDTH_B = len(POOL_WINDOWS) * POOL_GROUP_DIM
DILATED_CONFIGS = ((128, 1), (512, 4), (2048, 16))
HALF_WINDOW = 64
GRID_W = 64
NA_ROWS = 8
NA_COLS = 16
EPS = 1e-6
NEG = -1e30

LANES = 128
VMEM_LIMIT_BYTES = 56 * 1024 * 1024

PROJ_TM = 512
FFN_TM = 512
FFN_CHUNK = 256
A_TQ = 128
A_BAND = A_TQ + 2 * HALF_WINDOW
C_QROWS = 2
C_KROWS = 10
POOL_HALO = 8
COPY_ROWS = 512


def _params(semantics):
    return pltpu.CompilerParams(dimension_semantics=semantics,
                                vmem_limit_bytes=VMEM_LIMIT_BYTES)


def _resident(shape):
    nd = len(shape)
    return pl.BlockSpec(shape, lambda *_: (0,) * nd, pipeline_mode=pl.Buffered(1))


def _rms(x, g):
    ms = jnp.mean(x * x, axis=-1, keepdims=True)
    return (x * lax.rsqrt(ms + EPS)) * g


def _ada_kernel(c_ref, w_ref, b_ref, o_ref):
    c = c_ref[...]
    act = c * jax.nn.sigmoid(c)
    o_ref[...] = jnp.dot(act, w_ref[...], preferred_element_type=F32,
                         precision=lax.Precision.HIGHEST) + b_ref[...]


def _ada(c, w_ada, b_ada):
    depth, d, six_d = w_ada.shape
    b = c.shape[0]
    n_chunks = six_d // d
    return pl.pallas_call(
        _ada_kernel,
        grid=(depth, n_chunks),
        in_specs=[
            pl.BlockSpec((b, d), lambda l, j: (0, 0)),
            pl.BlockSpec((None, d, d), lambda l, j: (l, 0, j)),
            pl.BlockSpec((None, 1, d), lambda l, j: (l, 0, j)),
        ],
        out_specs=pl.BlockSpec((None, b, d), lambda l, j: (l, 0, j)),
        out_shape=jax.ShapeDtypeStruct((depth, b, six_d), F32),
        compiler_params=_params(("arbitrary", "arbitrary")),
        name="ada",
    )(c, w_ada, b_ada.reshape(depth, 1, six_d))


_PROJ_OUTS = (
    ("qa", WIDTH_A, HEAD_DIM ** -0.5, F32),
    ("ka", WIDTH_A, None, F32),
    ("va", WIDTH_A, None, F32),
    ("ub", WIDTH_B, None, F32),
    ("qc", WIDTH_C, HEAD_DIM ** -0.5, BF16),
    ("kc", WIDTH_C, None, BF16),
    ("vc", WIDTH_C, None, BF16),
)


def _proj_kernel(x_ref, g_ref, sc_ref, sh_ref, w_ref, *out_refs):
    h = _rms(x_ref[...], g_ref[...]) * (1.0 + sc_ref[...]) + sh_ref[...]
    h = h.astype(BF16)
    lo = 0
    for (_, width, scale, dtype), o_ref in zip(_PROJ_OUTS, out_refs):
        z = jnp.dot(h, w_ref[:, lo:lo + width], preferred_element_type=F32)
        if scale is not None:
            z = z * scale
        o_ref[...] = z.astype(dtype)
        lo += width


def _proj(x, g, sc, sh, w_in):
    b, s, d = x.shape
    tm = PROJ_TM
    vec = pl.BlockSpec((None, 1, d), lambda bi, i: (bi, 0, 0))
    return pl.pallas_call(
        _proj_kernel,
        grid=(b, s // tm),
        in_specs=[
            pl.BlockSpec((None, tm, d), lambda bi, i: (bi, i, 0)),
            pl.BlockSpec((1, d), lambda bi, i: (0, 0)),
            vec, vec,
            _resident(w_in.shape),
        ],
        out_specs=[pl.BlockSpec((None, tm, width), lambda bi, i: (bi, i, 0))
                   for _, width, _, _ in _PROJ_OUTS],
        out_shape=[jax.ShapeDtypeStruct((b, s, width), dtype)
                   for _, width, _, dtype in _PROJ_OUTS],
        compiler_params=_params(("arbitrary", "arbitrary")),
        name="proj",
    )(x, g.reshape(1, d), sc, sh, w_in)


def _softmax_pv(s, vb):
    m = jnp.max(s, axis=-1, keepdims=True)
    p = jnp.exp(s - m)
    den = jnp.sum(p, axis=-1, keepdims=True)
    pv = jnp.dot(p.astype(BF16), vb, preferred_element_type=F32)
    return pv / den, m + jnp.log(den)


def _attn_a_kernel(slopes_ref, q_ref, k_ref, v_ref, o_ref,
                   qs, ks, vs, tab, ob, lb, on, ln, *, seq):
    hp = pl.program_id(1)
    lane = lax.broadcasted_iota(jnp.int32, (1, LANES), 1)
    is_h0 = lane < HEAD_DIM
    pad = HALF_WINDOW

    zpad = jnp.zeros((pad, LANES), BF16)
    for buf in (ks, vs):
        buf[0:pad, :] = zpad
        buf[pad + seq:pad + seq + pad, :] = zpad

    qi = lax.broadcasted_iota(jnp.int32, (A_TQ, A_BAND), 0)
    kj = lax.broadcasted_iota(jnp.int32, (A_TQ, A_BAND), 1)
    arel = jnp.abs(kj - pad - qi)
    in_window = arel <= HALF_WINDOW

    for o, (_, dil) in enumerate(DILATED_CONFIGS):
        n = seq // dil
        nbk = n // A_TQ
        assert n % A_TQ == 0 and nbk >= 2

        dist = (arel * dil).astype(F32)
        for hh in range(2):
            slope = slopes_ref[2 * hp + hh]
            base = jnp.where(in_window, -(slope * dist), NEG)
            tab[hh] = base
            tab[2 + hh] = jnp.where(kj >= pad, base, NEG)
            tab[4 + hh] = jnp.where(kj < pad + A_TQ, base, NEG)

        rows = min(n, COPY_ROWS)

        def regroup(r, carry, n=n, dil=dil, rows=rows):
            for c in range(n // rows):
                src = pl.ds(r + c * rows * dil, rows, stride=dil) if dil > 1 else pl.ds(c * rows, rows)
                dst = pl.multiple_of(r * n + c * rows, rows)
                qs[pl.ds(dst, rows), :] = q_ref[src, :].astype(BF16)
                ks[pl.ds(pad + dst, rows), :] = k_ref[src, :].astype(BF16)
                vs[pl.ds(pad + dst, rows), :] = v_ref[src, :].astype(BF16)
            return carry

        lax.fori_loop(0, dil, regroup, 0)

        dst_o, dst_l = (on.at[0], ln.at[0]) if dil == 1 else (ob, lb)

        def block(jb, carry, nbk=nbk, dst_o=dst_o, dst_l=dst_l):
            row0 = pl.multiple_of(jb * A_TQ, A_TQ)
            local = jb % nbk
            var = jnp.where(local == 0, 1, jnp.where(local == nbk - 1, 2, 0))
            qb = qs[pl.ds(row0, A_TQ), :]
            kb = ks[pl.ds(row0, A_BAND), :]
            vb = vs[pl.ds(row0, A_BAND), :]
            outs, lses = [], []
            for hh in range(2):
                qh = jnp.where(is_h0 if hh == 0 else jnp.logical_not(is_h0), qb, jnp.zeros_like(qb))
                s = lax.dot_general(qh, kb, (((1,), (1,)), ((), ())), preferred_element_type=F32)
                out, lse = _softmax_pv(s + tab[2 * var + hh], vb)
                outs.append(out)
                lses.append(lse)
            dst_o[pl.ds(row0, A_TQ), :] = jnp.where(is_h0, outs[0], outs[1])
            dst_l[pl.ds(row0, A_TQ), :] = jnp.where(is_h0, lses[0], lses[1])
            return carry

        lax.fori_loop(0, seq // A_TQ, block, 0)

        if dil > 1:
            def ungroup(r, carry, o=o, n=n, dil=dil, rows=rows):
                for c in range(n // rows):
                    src = pl.ds(pl.multiple_of(r * n + c * rows, rows), rows)
                    dst = pl.ds(r + c * rows * dil, rows, stride=dil)
                    on[o, dst, :] = ob[src, :]
                    ln[o, dst, :] = lb[src, :]
                return carry

            lax.fori_loop(0, dil, ungroup, 0)

    def merge(c, carry):
        rows = pl.ds(pl.multiple_of(c * COPY_ROWS, COPY_ROWS), COPY_ROWS)
        l0, l1, l2 = ln[0, rows, :], ln[1, rows, :], ln[2, rows, :]
        m = jnp.maximum(jnp.maximum(l0, l1), l2)
        w0, w1, w2 = jnp.exp(l0 - m), jnp.exp(l1 - m), jnp.exp(l2 - m)
        num = w0 * on[0, rows, :] + w1 * on[1, rows, :] + w2 * on[2, rows, :]
        o_ref[rows, :] = num / (w0 + w1 + w2)
        return carry

    lax.fori_loop(0, seq // COPY_ROWS, merge, 0)


def _attn_a(qa, ka, va, slopes):
    b, s, width = qa.shape
    pairs = width // LANES
    blk = pl.BlockSpec((None, s, LANES), lambda bi, hp, *_: (bi, 0, hp))
    n_branch = len(DILATED_CONFIGS)
    return pl.pallas_call(
        functools.partial(_attn_a_kernel, seq=s),
        grid_spec=pltpu.PrefetchScalarGridSpec(
            num_scalar_prefetch=1,
            grid=(b, pairs),
            in_specs=[blk, blk, blk],
            out_specs=blk,
            scratch_shapes=[
                pltpu.VMEM((s, LANES), BF16),
                pltpu.VMEM((s + 2 * HALF_WINDOW, LANES), BF16),
                pltpu.VMEM((s + 2 * HALF_WINDOW, LANES), BF16),
                pltpu.VMEM((6, A_TQ, A_BAND), F32),
                pltpu.VMEM((s, LANES), F32),
                pltpu.VMEM((s, LANES), F32),
                pltpu.VMEM((n_branch, s, LANES), F32),
                pltpu.VMEM((n_branch, s, LANES), F32),
            ],
        ),
        out_shape=jax.ShapeDtypeStruct((b, s, width), F32),
        compiler_params=_params(("arbitrary", "arbitrary")),
        name="attn_dilated",
    )(slopes, qa, ka, va)


def _na_variant_rows(n_rows):
    return (NA_ROWS // 2, 0, C_QROWS, n_rows - 2 * C_QROWS, n_rows - C_QROWS)


def _na_bias_tables(rpb, n_rows):
    a = np.arange(C_QROWS).reshape(-1, 1, 1, 1)
    qc = np.arange(GRID_W).reshape(1, -1, 1, 1)
    i = np.arange(C_KROWS).reshape(1, 1, -1, 1)
    kc = np.arange(GRID_W).reshape(1, 1, 1, -1)
    cstart = np.clip(qc - NA_COLS // 2, 0, GRID_W - NA_COLS)
    col_in = (kc >= cstart) & (kc < cstart + NA_COLS)
    dc = np.clip(kc - qc + NA_COLS - 1, 0, 2 * NA_COLS - 2)
    tables = []
    for r in _na_variant_rows(n_rows):
        base = int(np.clip(r - NA_ROWS // 2, 0, n_rows - C_KROWS))
        row = r + a
        krow = base + i
        rstart = np.clip(row - NA_ROWS // 2, 0, n_rows - NA_ROWS)
        valid = (krow >= rstart) & (krow < rstart + NA_ROWS) & col_in
        dr = np.clip(krow - row + NA_ROWS - 1, 0, 2 * NA_ROWS - 2)
        shape = (C_QROWS, GRID_W, C_KROWS, GRID_W)
        dr_b, dc_b, valid_b = (np.broadcast_to(t, shape).reshape(C_QROWS * GRID_W, C_KROWS * GRID_W)
                               for t in (dr, dc, valid))
        tables.append(jnp.where(valid_b[None], rpb[:, dr_b, dc_b].astype(F32), NEG))
    return jnp.stack(tables, axis=1)


def _attn_c_kernel(q_ref, k_ref, v_ref, bias_ref, o_ref, *, n_rows):
    lane = lax.broadcasted_iota(jnp.int32, (1, LANES), 1)
    is_h0 = lane < HEAD_DIM
    n_blocks = n_rows // C_QROWS
    tq = C_QROWS * GRID_W
    tk = C_KROWS * GRID_W
    assert n_blocks >= 5

    def block(p, carry):
        r = p * C_QROWS
        base = jnp.clip(r - NA_ROWS // 2, 0, n_rows - C_KROWS)
        var = jnp.where(p == 0, 1, jnp.where(p == 1, 2, jnp.where(
            p == n_blocks - 2, 3, jnp.where(p == n_blocks - 1, 4, 0))))
        q0 = pl.multiple_of(p * tq, tq)
        k0 = pl.multiple_of(base * GRID_W, GRID_W)
        qb = q_ref[pl.ds(q0, tq), :]
        kb = k_ref[pl.ds(k0, tk), :]
        vb = v_ref[pl.ds(k0, tk), :]
        outs = []
        for hh in range(2):
            qh = jnp.where(is_h0 if hh == 0 else jnp.logical_not(is_h0), qb, jnp.zeros_like(qb))
            s = lax.dot_general(qh, kb, (((1,), (1,)), ((), ())), preferred_element_type=F32)
            out, _ = _softmax_pv(s + bias_ref[hh, var], vb)
            outs.append(out)
        o_ref[pl.ds(q0, tq), :] = jnp.where(is_h0, outs[0], outs[1])
        return carry

    lax.fori_loop(0, n_blocks, block, 0)


def _attn_c(qc, kc, vc, bias):
    b, s, width = qc.shape
    pairs = width // LANES
    n_var = bias.shape[1]
    blk = pl.BlockSpec((None, s, LANES), lambda bi, hp: (bi, 0, hp))
    return pl.pallas_call(
        functools.partial(_attn_c_kernel, n_rows=s // GRID_W),
        grid=(b, pairs),
        in_specs=[blk, blk, blk,
                  pl.BlockSpec((2, n_var) + bias.shape[2:], lambda bi, hp: (hp, 0, 0, 0))],
        out_specs=blk,
        out_shape=jax.ShapeDtypeStruct((b, s, width), F32),
        compiler_params=_params(("arbitrary", "arbitrary")),
        name="attn_neighbourhood",
    )(qc, kc, vc, bias)


def _mix_ffn_kernel(oa_ref, oc_ref, u_ref, up_ref, un_ref, x_ref,
                    na_ref, nc_ref, wpool_ref, ps_ref, wout_ref, g1_ref,
                    nf_ref, sc2_ref, sh2_ref, w1_ref, w2_ref, g2_ref, nfin_ref,
                    o_ref, cat, act, ext, *, tm, seq, d_ff, final_norm):
    i = pl.program_id(1)
    halo = POOL_HALO

    u = u_ref[...]
    width_b = u.shape[-1]
    ext[0:halo, :] = jnp.where(i > 0, up_ref[...], 0.0)
    ext[halo:halo + tm, :] = u
    ext[halo + tm:halo + tm + halo, :] = jnp.where(i < pl.num_programs(1) - 1, un_ref[...], 0.0)
    t = i * tm + lax.broadcasted_iota(jnp.int32, (tm, 1), 0)
    lane = lax.broadcasted_iota(jnp.int32, (1, width_b), 1)
    num = jnp.zeros((tm, width_b), F32)
    den = jnp.zeros((tm, width_b), F32)
    acc = jnp.zeros((tm, width_b), F32)
    done = 0
    for g, w in enumerate(POOL_WINDOWS):
        for d in list(range(-(w // 2), -done)) + list(range(done, w // 2)):
            acc = acc + ext[halo + d:halo + d + tm, :]
        done = w // 2
        lo = jnp.clip(t - w // 2, 0, seq - 1)
        hi = jnp.clip(t + w // 2 - 1, 0, seq - 1)
        cnt = (hi - lo + 1).astype(F32)
        in_group = (lane >= g * POOL_GROUP_DIM) & (lane < (g + 1) * POOL_GROUP_DIM)
        num = jnp.where(in_group, acc, num)
        den = jnp.where(in_group, cnt, den)
    pooled = num / den - u
    y = jnp.dot(pooled.astype(BF16), wpool_ref[...], preferred_element_type=F32) * ps_ref[...]

    wa = oa_ref.shape[-1]
    cat[:, 0:wa] = _rms(oa_ref[...], na_ref[...]).astype(BF16)
    cat[:, wa:wa + width_b] = y.astype(BF16)
    cat[:, wa + width_b:] = _rms(oc_ref[...], nc_ref[...]).astype(BF16)
    mix = jnp.dot(cat[...], wout_ref[...], preferred_element_type=F32)
    x1 = x_ref[...] + g1_ref[...] * mix

    h2 = (_rms(x1, nf_ref[...]) * (1.0 + sc2_ref[...]) + sh2_ref[...]).astype(BF16)
    for c in range(d_ff // FFN_CHUNK):
        lo = c * FFN_CHUNK
        gate = jnp.dot(h2, w1_ref[:, lo:lo + FFN_CHUNK], preferred_element_type=F32)
        up = jnp.dot(h2, w1_ref[:, d_ff + lo:d_ff + lo + FFN_CHUNK], preferred_element_type=F32)
        act[:, lo:lo + FFN_CHUNK] = (gate * jax.nn.sigmoid(gate) * up).astype(BF16)
    ffn = jnp.dot(act[...], w2_ref[...], preferred_element_type=F32)
    x2 = x1 + g2_ref[...] * ffn
    if final_norm:
        x2 = _rms(x2, nfin_ref[...])
    o_ref[...] = x2


def _mix_ffn(oa, oc, ub, x, norm_a, norm_c, w_pool_bd, pool_scale, w_out, g1,
             norm_ffn, sc2, sh2, w1, w2, g2, norm_final, final_norm):
    b, s, d = x.shape
    tm = FFN_TM
    d_ff = w2.shape[0]
    width_b = ub.shape[-1]
    assert d_ff % FFN_CHUNK == 0 and tm % POOL_HALO == 0
    hb = tm // POOL_HALO
    n_hb = s // POOL_HALO
    tile = lambda width: pl.BlockSpec((None, tm, width), lambda bi, i: (bi, i, 0))
    row = lambda width: pl.BlockSpec((1, width), lambda bi, i: (0, 0))
    vec = pl.BlockSpec((None, 1, d), lambda bi, i: (bi, 0, 0))
    return pl.pallas_call(
        functools.partial(_mix_ffn_kernel, tm=tm, seq=s, d_ff=d_ff, final_norm=final_norm),
        grid=(b, s // tm),
        in_specs=[
            tile(oa.shape[-1]), tile(oc.shape[-1]), tile(width_b),
            pl.BlockSpec((None, POOL_HALO, width_b),
                         lambda bi, i: (bi, jnp.maximum(i * hb - 1, 0), 0)),
            pl.BlockSpec((None, POOL_HALO, width_b),
                         lambda bi, i: (bi, jnp.minimum((i + 1) * hb, n_hb - 1), 0)),
            tile(d),
            row(oa.shape[-1]), row(oc.shape[-1]),
            _resident(w_pool_bd.shape), row(width_b), _resident(w_out.shape), vec,
            row(d), vec, vec, _resident(w1.shape), _resident(w2.shape), vec, row(d),
        ],
        out_specs=tile(d),
        out_shape=jax.ShapeDtypeStruct((b, s, d), F32),
        scratch_shapes=[
            pltpu.VMEM((tm, d), BF16),
            pltpu.VMEM((tm, d_ff), BF16),
            pltpu.VMEM((tm + 2 * POOL_HALO, width_b), F32),
        ],
        compiler_params=_params(("arbitrary", "arbitrary")),
        name="mix_ffn",
    )(oa, oc, ub, ub, ub, x,
      norm_a.reshape(1, -1), norm_c.reshape(1, -1), w_pool_bd, pool_scale.reshape(1, -1),
      w_out, g1, norm_ffn.reshape(1, -1), sc2, sh2, w1, w2, g2, norm_final.reshape(1, -1))


def _block_diag(w_pool):
    g, gd, _ = w_pool.shape
    out = jnp.zeros((g * gd, g * gd), w_pool.dtype)
    for j in range(g):
        out = out.at[j * gd:(j + 1) * gd, j * gd:(j + 1) * gd].set(w_pool[j])
    return out


def kernel(x, c, w_ada, b_ada, norm_mix, w_in, norm_a_out, norm_c_out, w_pool, pool_scale, rpb,
           w_out, norm_ffn, w_ffn_in, w_ffn_out, norm_final):
    depth = w_ada.shape[0]
    b, s, d = x.shape
    assert s % GRID_W == 0 and s % PROJ_TM == 0 and s % FFN_TM == 0 and s % COPY_ROWS == 0

    mod = _ada(c, w_ada, b_ada)
    mod = mod.reshape(depth, b, 6, 1, d)
    heads = jnp.arange(N_HEADS_A, dtype=F32)
    slopes = 2.0 ** (-8.0 * (heads + 1.0) / N_HEADS_A)

    for l in range(depth):
        sh1, sc1, g1, sh2, sc2, g2 = (mod[l, :, j] for j in range(6))
        qa, ka, va, ub, qc, kc, vc = _proj(x, norm_mix[l], sc1, sh1, w_in[l].astype(BF16))
        oa = _attn_a(qa, ka, va, slopes)
        oc = _attn_c(qc, kc, vc, _na_bias_tables(rpb[l], s // GRID_W))
        x = _mix_ffn(oa, oc, ub, x, norm_a_out[l], norm_c_out[l],
                     _block_diag(w_pool[l]).astype(BF16), pool_scale[l],
                     w_out[l].astype(BF16), g1, norm_ffn[l], sc2, sh2,
                     w_ffn_in[l].astype(BF16), w_ffn_out[l].astype(BF16), g2,
                     norm_final, final_norm=(l == depth - 1))
    return x
```

```python
import functools

import numpy as np
import jax
import jax.numpy as jnp
from jax import lax
from jax.experimental import pallas as pl
from jax.experimental.pallas import tpu as pltpu

F32 = jnp.float32
BF16 = jnp.bfloat16

HEAD_DIM = 64
N_HEADS_A = 6
N_HEADS_C = 6
WIDTH_A = N_HEADS_A * HEAD_DIM
WIDTH_C = N_HEADS_C * HEAD_DIM
POOL_WINDOWS = (2, 4, 8, 16)
POOL_GROUP_DIM = 64
WIDTH_B = len(POOL_WINDOWS) * POOL_GROUP_DIM
DILATED_CONFIGS = ((128, 1), (512, 4), (2048, 16))
HALF_WINDOW = 64
GRID_W = 64
NA_ROWS = 8
NA_COLS = 16
EPS = 1e-6
NEG = -1e30

LANES = 128
VMEM_LIMIT_BYTES = 56 * 1024 * 1024

PROJ_TM = 512
FFN_TM = 512
FFN_CHUNK = 256
A_TQ = 128
A_BAND = A_TQ + 2 * HALF_WINDOW
A_UNROLL = 4
C_UNROLL = 4
C_QROWS = 2
C_KROWS = 10
POOL_HALO = 8
COPY_ROWS = 512


def _params(semantics):
    return pltpu.CompilerParams(dimension_semantics=semantics,
                                vmem_limit_bytes=VMEM_LIMIT_BYTES)


def _resident(shape):
    nd = len(shape)
    return pl.BlockSpec(shape, lambda *_: (0,) * nd, pipeline_mode=pl.Buffered(1))


def _rms(x, g):
    ms = jnp.mean(x * x, axis=-1, keepdims=True)
    return (x * lax.rsqrt(ms + EPS)) * g


def _ada_kernel(c_ref, w_ref, b_ref, o_ref):
    c = c_ref[...]
    act = c * jax.nn.sigmoid(c)
    o_ref[...] = jnp.dot(act, w_ref[...], preferred_element_type=F32,
                         precision=lax.Precision.HIGHEST) + b_ref[...]


def _ada(c, w_ada, b_ada):
    depth, d, six_d = w_ada.shape
    b = c.shape[0]
    n_chunks = six_d // d
    return pl.pallas_call(
        _ada_kernel,
        grid=(depth, n_chunks),
        in_specs=[
            pl.BlockSpec((b, d), lambda l, j: (0, 0)),
            pl.BlockSpec((None, d, d), lambda l, j: (l, 0, j)),
            pl.BlockSpec((None, 1, d), lambda l, j: (l, 0, j)),
        ],
        out_specs=pl.BlockSpec((None, b, d), lambda l, j: (l, 0, j)),
        out_shape=jax.ShapeDtypeStruct((depth, b, six_d), F32),
        compiler_params=_params(("arbitrary", "arbitrary")),
        name="ada",
    )(c, w_ada, b_ada.reshape(depth, 1, six_d))


_PROJ_OUTS = (
    ("qa", WIDTH_A, HEAD_DIM ** -0.5, F32),
    ("ka", WIDTH_A, None, F32),
    ("va", WIDTH_A, None, F32),
    ("ub", WIDTH_B, None, F32),
    ("qc", WIDTH_C, HEAD_DIM ** -0.5, BF16),
    ("kc", WIDTH_C, None, BF16),
    ("vc", WIDTH_C, None, BF16),
)


def _proj_kernel(x_ref, g_ref, sc_ref, sh_ref, w_ref, *out_refs):
    h = _rms(x_ref[...], g_ref[...]) * (1.0 + sc_ref[...]) + sh_ref[...]
    h = h.astype(BF16)
    lo = 0
    for (_, width, scale, dtype), o_ref in zip(_PROJ_OUTS, out_refs):
        z = jnp.dot(h, w_ref[:, lo:lo + width], preferred_element_type=F32)
        if scale is not None:
            z = z * scale
        o_ref[...] = z.astype(dtype)
        lo += width


def _proj(x, g, sc, sh, w_in):
    b, s, d = x.shape
    tm = PROJ_TM
    vec = pl.BlockSpec((None, 1, d), lambda bi, i: (bi, 0, 0))
    return pl.pallas_call(
        _proj_kernel,
        grid=(b, s // tm),
        in_specs=[
            pl.BlockSpec((None, tm, d), lambda bi, i: (bi, i, 0)),
            pl.BlockSpec((1, d), lambda bi, i: (0, 0)),
            vec, vec,
            _resident(w_in.shape),
        ],
        out_specs=[pl.BlockSpec((None, tm, width), lambda bi, i: (bi, i, 0))
                   for _, width, _, _ in _PROJ_OUTS],
        out_shape=[jax.ShapeDtypeStruct((b, s, width), dtype)
                   for _, width, _, dtype in _PROJ_OUTS],
        compiler_params=_params(("arbitrary", "arbitrary")),
        name="proj",
    )(x, g.reshape(1, d), sc, sh, w_in)


def _stack_heads(qb, is_h0):
    zero = jnp.zeros_like(qb)
    return jnp.concatenate([jnp.where(is_h0, qb, zero), jnp.where(is_h0, zero, qb)], axis=0)


def _pair_attention(q2, kb, vb, bias, is_h0):
    rows = q2.shape[0] // 2
    s = lax.dot_general(q2, kb, (((1,), (1,)), ((), ())), preferred_element_type=F32) + bias
    m = jnp.max(s, axis=-1, keepdims=True)
    p = jnp.exp(s - m)
    den = jnp.sum(p, axis=-1, keepdims=True)
    pv = jnp.dot(p.astype(BF16), vb, preferred_element_type=F32)
    out = jnp.where(is_h0, pv[:rows], pv[rows:]) / jnp.where(is_h0, den[:rows], den[rows:])
    lse = m + jnp.log(den)
    return out, jnp.where(is_h0, lse[:rows], lse[rows:])


def _attn_a_kernel(slopes_ref, q_ref, k_ref, v_ref, o_ref,
                   qs, ks, vs, tab, ob, lb, on, ln, *, seq):
    hp = pl.program_id(1)
    lane = lax.broadcasted_iota(jnp.int32, (1, LANES), 1)
    is_h0 = lane < HEAD_DIM
    pad = HALF_WINDOW

    zpad = jnp.zeros((pad, LANES), BF16)
    for buf in (ks, vs):
        buf[0:pad, :] = zpad
        buf[pad + seq:pad + seq + pad, :] = zpad

    qi = lax.broadcasted_iota(jnp.int32, (A_TQ, A_BAND), 0)
    kj = lax.broadcasted_iota(jnp.int32, (A_TQ, A_BAND), 1)
    arel = jnp.abs(kj - pad - qi)
    in_window = arel <= HALF_WINDOW

    for o, (_, dil) in enumerate(DILATED_CONFIGS):
        n = seq // dil
        nbk = n // A_TQ
        assert n % A_TQ == 0 and nbk >= 2

        dist = (arel * dil).astype(F32)
        for hh in range(2):
            slope = slopes_ref[2 * hp + hh]
            base = jnp.where(in_window, -(slope * dist), NEG)
            head_rows = pl.ds(hh * A_TQ, A_TQ)
            tab[0, head_rows, :] = base
            tab[1, head_rows, :] = jnp.where(kj >= pad, base, NEG)
            tab[2, head_rows, :] = jnp.where(kj < pad + A_TQ, base, NEG)

        rows = min(n, COPY_ROWS)

        def regroup(r, carry, n=n, dil=dil, rows=rows):
            for c in range(n // rows):
                src = pl.ds(r + c * rows * dil, rows, stride=dil) if dil > 1 else pl.ds(c * rows, rows)
                dst = pl.multiple_of(r * n + c * rows, rows)
                qs[pl.ds(dst, rows), :] = q_ref[src, :].astype(BF16)
                ks[pl.ds(pad + dst, rows), :] = k_ref[src, :].astype(BF16)
                vs[pl.ds(pad + dst, rows), :] = v_ref[src, :].astype(BF16)
            return carry

        lax.fori_loop(0, dil, regroup, 0)

        dst_o, dst_l = (on.at[0], ln.at[0]) if dil == 1 else (ob, lb)

        def block(jb, carry, nbk=nbk, dst_o=dst_o, dst_l=dst_l):
            row0 = pl.multiple_of(jb * A_TQ, A_TQ)
            local = jb % nbk
            var = jnp.where(local == 0, 1, jnp.where(local == nbk - 1, 2, 0))
            qb = qs[pl.ds(row0, A_TQ), :]
            kb = ks[pl.ds(row0, A_BAND), :]
            vb = vs[pl.ds(row0, A_BAND), :]
            out, lse = _pair_attention(_stack_heads(qb, is_h0), kb, vb, tab[var], is_h0)
            dst_o[pl.ds(row0, A_TQ), :] = out
            dst_l[pl.ds(row0, A_TQ), :] = lse
            return carry

        lax.fori_loop(0, seq // A_TQ, block, 0, unroll=A_UNROLL)

        if dil > 1:
            def ungroup(r, carry, o=o, n=n, dil=dil, rows=rows):
                for c in range(n // rows):
                    src = pl.ds(pl.multiple_of(r * n + c * rows, rows), rows)
                    dst = pl.ds(r + c * rows * dil, rows, stride=dil)
                    on[o, dst, :] = ob[src, :]
                    ln[o, dst, :] = lb[src, :]
                return carry

            lax.fori_loop(0, dil, ungroup, 0)

    def merge(c, carry):
        rows = pl.ds(pl.multiple_of(c * COPY_ROWS, COPY_ROWS), COPY_ROWS)
        l0, l1, l2 = ln[0, rows, :], ln[1, rows, :], ln[2, rows, :]
        m = jnp.maximum(jnp.maximum(l0, l1), l2)
        w0, w1, w2 = jnp.exp(l0 - m), jnp.exp(l1 - m), jnp.exp(l2 - m)
        num = w0 * on[0, rows, :] + w1 * on[1, rows, :] + w2 * on[2, rows, :]
        o_ref[rows, :] = num / (w0 + w1 + w2)
        return carry

    lax.fori_loop(0, seq // COPY_ROWS, merge, 0)


def _attn_a(qa, ka, va, slopes):
    b, s, width = qa.shape
    pairs = width // LANES
    blk = pl.BlockSpec((None, s, LANES), lambda bi, hp, *_: (bi, 0, hp))
    n_branch = len(DILATED_CONFIGS)
    return pl.pallas_call(
        functools.partial(_attn_a_kernel, seq=s),
        grid_spec=pltpu.PrefetchScalarGridSpec(
            num_scalar_prefetch=1,
            grid=(b, pairs),
            in_specs=[blk, blk, blk],
            out_specs=blk,
            scratch_shapes=[
                pltpu.VMEM((s, LANES), BF16),
                pltpu.VMEM((s + 2 * HALF_WINDOW, LANES), BF16),
                pltpu.VMEM((s + 2 * HALF_WINDOW, LANES), BF16),
                pltpu.VMEM((3, 2 * A_TQ, A_BAND), F32),
                pltpu.VMEM((s, LANES), F32),
                pltpu.VMEM((s, LANES), F32),
                pltpu.VMEM((n_branch, s, LANES), F32),
                pltpu.VMEM((n_branch, s, LANES), F32),
            ],
        ),
        out_shape=jax.ShapeDtypeStruct((b, s, width), F32),
        compiler_params=_params(("arbitrary", "arbitrary")),
        name="attn_dilated",
    )(slopes, qa, ka, va)


def _na_variant_rows(n_rows):
    return (NA_ROWS // 2, 0, C_QROWS, n_rows - 2 * C_QROWS, n_rows - C_QROWS)


def _na_bias_tables(rpb, n_rows):
    n_heads, n_dr, n_dc = rpb.shape
    w = GRID_W
    left = w - 1 - (NA_COLS - 1)
    padded = jnp.pad(rpb.astype(F32), ((0, 0), (0, 0), (left, 2 * w - left - n_dc)),
                     constant_values=NEG)
    skew = jnp.broadcast_to(padded[:, :, None, :], (n_heads, n_dr, w, 2 * w))
    skew = skew.reshape(n_heads, n_dr, 2 * w * w)[:, :, :w * (2 * w - 1)]
    toep = skew.reshape(n_heads, n_dr, w, 2 * w - 1)[:, :, :, w - 1:]
    qc = np.arange(w).reshape(-1, 1)
    kc = np.arange(w).reshape(1, -1)
    cstart = np.clip(qc - NA_COLS // 2, 0, w - NA_COLS)
    col_in = (kc >= cstart) & (kc < cstart + NA_COLS)
    toep = jnp.where(col_in, toep, NEG)
    masked = jnp.full((n_heads, w, w), NEG, F32)
    tables = []
    for r in _na_variant_rows(n_rows):
        base = int(np.clip(r - NA_ROWS // 2, 0, n_rows - C_KROWS))
        q_rows = []
        for a in range(C_QROWS):
            row = r + a
            rstart = int(np.clip(row - NA_ROWS // 2, 0, n_rows - NA_ROWS))
            tiles = []
            for i in range(C_KROWS):
                krow = base + i
                valid = rstart <= krow < rstart + NA_ROWS
                tiles.append(toep[:, krow - row + NA_ROWS - 1] if valid else masked)
            q_rows.append(jnp.concatenate(tiles, axis=-1))
        table = jnp.concatenate(q_rows, axis=-2)
        tables.append(table.reshape(n_heads // 2, 2 * C_QROWS * w, C_KROWS * w))
    return jnp.stack(tables, axis=1)


def _attn_c_kernel(q_ref, k_ref, v_ref, bias_ref, o_ref, *, n_rows):
    lane = lax.broadcasted_iota(jnp.int32, (1, LANES), 1)
    is_h0 = lane < HEAD_DIM
    n_blocks = n_rows // C_QROWS
    tq = C_QROWS * GRID_W
    tk = C_KROWS * GRID_W
    assert n_blocks >= 5

    def block(p, carry):
        r = p * C_QROWS
        base = jnp.clip(r - NA_ROWS // 2, 0, n_rows - C_KROWS)
        var = jnp.where(p == 0, 1, jnp.where(p == 1, 2, jnp.where(
            p == n_blocks - 2, 3, jnp.where(p == n_blocks - 1, 4, 0))))
        q0 = pl.multiple_of(p * tq, tq)
        k0 = pl.multiple_of(base * GRID_W, GRID_W)
        qb = q_ref[pl.ds(q0, tq), :]
        kb = k_ref[pl.ds(k0, tk), :]
        vb = v_ref[pl.ds(k0, tk), :]
        out, _ = _pair_attention(_stack_heads(qb, is_h0), kb, vb, bias_ref[var], is_h0)
        o_ref[pl.ds(q0, tq), :] = out
        return carry

    lax.fori_loop(0, n_blocks, block, 0, unroll=C_UNROLL)


def _attn_c(qc, kc, vc, bias):
    b, s, width = qc.shape
    pairs = width // LANES
    blk = pl.BlockSpec((None, s, LANES), lambda bi, hp: (bi, 0, hp))
    return pl.pallas_call(
        functools.partial(_attn_c_kernel, n_rows=s // GRID_W),
        grid=(b, pairs),
        in_specs=[blk, blk, blk,
                  pl.BlockSpec((None,) + bias.shape[1:], lambda bi, hp: (hp, 0, 0, 0))],
        out_specs=blk,
        out_shape=jax.ShapeDtypeStruct((b, s, width), F32),
        compiler_params=_params(("arbitrary", "arbitrary")),
        name="attn_neighbourhood",
    )(qc, kc, vc, bias)


def _mix_ffn_kernel(oa_ref, oc_ref, u_ref, up_ref, un_ref, x_ref,
                    na_ref, nc_ref, wpool_ref, ps_ref, wout_ref, g1_ref,
                    nf_ref, sc2_ref, sh2_ref, w1_ref, w2_ref, g2_ref, nfin_ref,
                    o_ref, cat, act, ext, *, tm, seq, d_ff, final_norm):
    i = pl.program_id(1)
    halo = POOL_HALO

    u = u_ref[...]
    width_b = u.shape[-1]
    ext[0:halo, :] = jnp.where(i > 0, up_ref[...], 0.0)
    ext[halo:halo + tm, :] = u
    ext[halo + tm:halo + tm + halo, :] = jnp.where(i < pl.num_programs(1) - 1, un_ref[...], 0.0)
    t = i * tm + lax.broadcasted_iota(jnp.int32, (tm, 1), 0)
    lane = lax.broadcasted_iota(jnp.int32, (1, width_b), 1)
    num = jnp.zeros((tm, width_b), F32)
    den = jnp.zeros((tm, width_b), F32)
    acc = jnp.zeros((tm, width_b), F32)
    done = 0
    for g, w in enumerate(POOL_WINDOWS):
        for d in list(range(-(w // 2), -done)) + list(range(done, w // 2)):
            acc = acc + ext[halo + d:halo + d + tm, :]
        done = w // 2
        lo = jnp.clip(t - w // 2, 0, seq - 1)
        hi = jnp.clip(t + w // 2 - 1, 0, seq - 1)
        cnt = (hi - lo + 1).astype(F32)
        in_group = (lane >= g * POOL_GROUP_DIM) & (lane < (g + 1) * POOL_GROUP_DIM)
        num = jnp.where(in_group, acc, num)
        den = jnp.where(in_group, cnt, den)
    pooled = num / den - u
    y = jnp.dot(pooled.astype(BF16), wpool_ref[...], preferred_element_type=F32) * ps_ref[...]

    wa = oa_ref.shape[-1]
    cat[:, 0:wa] = _rms(oa_ref[...], na_ref[...]).astype(BF16)
    cat[:, wa:wa + width_b] = y.astype(BF16)
    cat[:, wa + width_b:] = _rms(oc_ref[...], nc_ref[...]).astype(BF16)
    mix = jnp.dot(cat[...], wout_ref[...], preferred_element_type=F32)
    x1 = x_ref[...] + g1_ref[...] * mix

    h2 = (_rms(x1, nf_ref[...]) * (1.0 + sc2_ref[...]) + sh2_ref[...]).astype(BF16)
    for c in range(d_ff // FFN_CHUNK):
        lo = c * FFN_CHUNK
        gate = jnp.dot(h2, w1_ref[:, lo:lo + FFN_CHUNK], preferred_element_type=F32)
        up = jnp.dot(h2, w1_ref[:, d_ff + lo:d_ff + lo + FFN_CHUNK], preferred_element_type=F32)
        act[:, lo:lo + FFN_CHUNK] = (gate * jax.nn.sigmoid(gate) * up).astype(BF16)
    ffn = jnp.dot(act[...], w2_ref[...], preferred_element_type=F32)
    x2 = x1 + g2_ref[...] * ffn
    if final_norm:
        x2 = _rms(x2, nfin_ref[...])
    o_ref[...] = x2


def _mix_ffn(oa, oc, ub, x, norm_a, norm_c, w_pool_bd, pool_scale, w_out, g1,
             norm_ffn, sc2, sh2, w1, w2, g2, norm_final, final_norm):
    b, s, d = x.shape
    tm = FFN_TM
    d_ff = w2.shape[0]
    width_b = ub.shape[-1]
    assert d_ff % FFN_CHUNK == 0 and tm % POOL_HALO == 0
    hb = tm // POOL_HALO
    n_hb = s // POOL_HALO
    tile = lambda width: pl.BlockSpec((None, tm, width), lambda bi, i: (bi, i, 0))
    row = lambda width: pl.BlockSpec((1, width), lambda bi, i: (0, 0))
    vec = pl.BlockSpec((None, 1, d), lambda bi, i: (bi, 0, 0))
    return pl.pallas_call(
        functools.partial(_mix_ffn_kernel, tm=tm, seq=s, d_ff=d_ff, final_norm=final_norm),
        grid=(b, s // tm),
        in_specs=[
            tile(oa.shape[-1]), tile(oc.shape[-1]), tile(width_b),
            pl.BlockSpec((None, POOL_HALO, width_b),
                         lambda bi, i: (bi, jnp.maximum(i * hb - 1, 0), 0)),
            pl.BlockSpec((None, POOL_HALO, width_b),
                         lambda bi, i: (bi, jnp.minimum((i + 1) * hb, n_hb - 1), 0)),
            tile(d),
            row(oa.shape[-1]), row(oc.shape[-1]),
            _resident(w_pool_bd.shape), row(width_b), _resident(w_out.shape), vec,
            row(d), vec, vec, _resident(w1.shape), _resident(w2.shape), vec, row(d),
        ],
        out_specs=tile(d),
        out_shape=jax.ShapeDtypeStruct((b, s, d), F32),
        scratch_shapes=[
            pltpu.VMEM((tm, d), BF16),
            pltpu.VMEM((tm, d_ff), BF16),
            pltpu.VMEM((tm + 2 * POOL_HALO, width_b), F32),
        ],
        compiler_params=_params(("arbitrary", "arbitrary")),
        name="mix_ffn",
    )(oa, oc, ub, ub, ub, x,
      norm_a.reshape(1, -1), norm_c.reshape(1, -1), w_pool_bd, pool_scale.reshape(1, -1),
      w_out, g1, norm_ffn.reshape(1, -1), sc2, sh2, w1, w2, g2, norm_final.reshape(1, -1))


def _block_diag(w_pool):
    g, gd, _ = w_pool.shape
    out = jnp.zeros((g * gd, g * gd), w_pool.dtype)
    for j in range(g):
        out = out.at[j * gd:(j + 1) * gd, j * gd:(j + 1) * gd].set(w_pool[j])
    return out


def kernel(x, c, w_ada, b_ada, norm_mix, w_in, norm_a_out, norm_c_out, w_pool, pool_scale, rpb,
           w_out, norm_ffn, w_ffn_in, w_ffn_out, norm_final):
    depth = w_ada.shape[0]
    b, s, d = x.shape
    assert s % GRID_W == 0 and s % PROJ_TM == 0 and s % FFN_TM == 0 and s % COPY_ROWS == 0

    mod = _ada(c, w_ada, b_ada)
    mod = mod.reshape(depth, b, 6, 1, d)
    heads = jnp.arange(N_HEADS_A, dtype=F32)
    slopes = 2.0 ** (-8.0 * (heads + 1.0) / N_HEADS_A)

    for l in range(depth):
        sh1, sc1, g1, sh2, sc2, g2 = (mod[l, :, j] for j in range(6))
        qa, ka, va, ub, qc, kc, vc = _proj(x, norm_mix[l], sc1, sh1, w_in[l].astype(BF16))
        oa = _attn_a(qa, ka, va, slopes)
        oc = _attn_c(qc, kc, vc, _na_bias_tables(rpb[l], s // GRID_W))
        x = _mix_ffn(oa, oc, ub, x, norm_a_out[l], norm_c_out[l],
                     _block_diag(w_pool[l]).astype(BF16), pool_scale[l],
                     w_out[l].astype(BF16), g1, norm_ffn[l], sc2, sh2,
                     w_ffn_in[l].astype(BF16), w_ffn_out[l].astype(BF16), g2,
                     norm_final, final_norm=(l == depth - 1))
    return x
```

```python
import functools

import numpy as np
import jax
import jax.numpy as jnp
from jax import lax
from jax.experimental import pallas as pl
from jax.experimental.pallas import tpu as pltpu

F32 = jnp.float32
BF16 = jnp.bfloat16

HEAD_DIM = 64
N_HEADS_A = 6
N_HEADS_C = 6
WIDTH_A = N_HEADS_A * HEAD_DIM
WIDTH_C = N_HEADS_C * HEAD_DIM
POOL_WINDOWS = (2, 4, 8, 16)
POOL_GROUP_DIM = 64
WIDTH_B = len(POOL_WINDOWS) * POOL_GROUP_DIM
DILATED_CONFIGS = ((128, 1), (512, 4), (2048, 16))
HALF_WINDOW = 64
GRID_W = 64
NA_ROWS = 8
NA_COLS = 16
N_MOD = 6
EPS = 1e-6
NEG = -1e30

LANES = 128
VMEM_LIMIT_BYTES = 56 * 1024 * 1024

PROJ_TM = 512
PROJ_CHUNK = 512
FFN_TM = 512
FFN_CHUNK = 256
A_TQ = 128
A_BAND = A_TQ + 2 * HALF_WINDOW
A_GROUP = 4
C_QROWS = 2
C_KROWS = 10
C_GROUP = 2
POOL_HALO = 8
COPY_ROWS = 512


def _params(semantics):
    return pltpu.CompilerParams(dimension_semantics=semantics,
                                vmem_limit_bytes=VMEM_LIMIT_BYTES)


def _layer_resident(shape, layer):
    zeros = (0,) * (len(shape) - 1)
    return pl.BlockSpec((None,) + tuple(shape[1:]), lambda *_: (layer,) + zeros,
                        pipeline_mode=pl.Buffered(1))


def _layer_row(width, layer):
    return pl.BlockSpec((None, 1, width), lambda *_: (layer, 0, 0))


def _mod_spec(d, layer, j):
    return pl.BlockSpec((None, None, None, 1, d), lambda bi, i: (layer, bi, j, 0, 0))


def _rms(x, g):
    ms = jnp.mean(x * x, axis=-1, keepdims=True)
    return (x * lax.rsqrt(ms + EPS)) * g


def _ada_kernel(c_ref, w_ref, b_ref, o_ref):
    c = c_ref[...]
    act = c * jax.nn.sigmoid(c)
    o_ref[...] = jnp.dot(act, w_ref[...], preferred_element_type=F32,
                         precision=lax.Precision.HIGHEST) + b_ref[...]


def _ada(c, w_ada, b_ada):
    depth, d, six_d = w_ada.shape
    b = c.shape[0]
    n_chunks = six_d // d
    return pl.pallas_call(
        _ada_kernel,
        grid=(depth, n_chunks),
        in_specs=[
            pl.BlockSpec((b, d), lambda l, j: (0, 0)),
            pl.BlockSpec((None, d, d), lambda l, j: (l, 0, j)),
            pl.BlockSpec((None, 1, d), lambda l, j: (l, 0, j)),
        ],
        out_specs=pl.BlockSpec((None, b, d), lambda l, j: (l, 0, j)),
        out_shape=jax.ShapeDtypeStruct((depth, b, six_d), F32),
        compiler_params=_params(("arbitrary", "arbitrary")),
        name="ada",
    )(c, w_ada, b_ada.reshape(depth, 1, six_d))


_PROJ_OUTS = (
    ("qa", WIDTH_A, HEAD_DIM ** -0.5, BF16),
    ("ka", WIDTH_A, None, BF16),
    ("va", WIDTH_A, None, BF16),
    ("ub", WIDTH_B, None, F32),
    ("qc", WIDTH_C, HEAD_DIM ** -0.5, BF16),
    ("kc", WIDTH_C, None, BF16),
    ("vc", WIDTH_C, None, BF16),
)
N_REGROUPED = 3
SUB_DIL = 4
assert tuple(dil for _, dil in DILATED_CONFIGS) == (1, SUB_DIL, SUB_DIL ** 2)


def _proj_kernel(x_ref, g_ref, sc_ref, sh_ref, w_ref, *refs):
    n_out = len(_PROJ_OUTS)
    out_refs = refs[:n_out]
    o4_refs = refs[n_out:n_out + N_REGROUPED]
    o16_refs = refs[n_out + N_REGROUPED:n_out + 2 * N_REGROUPED]
    zs, z4 = refs[n_out + 2 * N_REGROUPED:]
    tm = x_ref.shape[0]

    h = _rms(x_ref[...], g_ref[...]) * (1.0 + sc_ref[...]) + sh_ref[...]
    h = h.astype(BF16)
    total = w_ref.shape[-1]
    for c0 in range(0, total, PROJ_CHUNK):
        c1 = min(c0 + PROJ_CHUNK, total)
        z = jnp.dot(h, w_ref[:, c0:c1], preferred_element_type=F32)
        lo = 0
        for idx, ((_, width, scale, dtype), o_ref) in enumerate(zip(_PROJ_OUTS, out_refs)):
            a, b = max(lo, c0), min(lo + width, c1)
            if a < b:
                piece = z[:, a - c0:b - c0]
                if scale is not None:
                    piece = piece * scale
                o_ref[:, a - lo:b - lo] = piece.astype(dtype)
                if idx < N_REGROUPED:
                    for col in range(a, b, LANES):
                        zs[col // LANES] = piece[:, col - a:col - a + LANES]
            lo += width

    n4, n16 = tm // SUB_DIL, tm // SUB_DIL ** 2
    tiles_per_out = WIDTH_A // LANES
    for j in range(N_REGROUPED * tiles_per_out):
        o4, o16 = o4_refs[j // tiles_per_out], o16_refs[j // tiles_per_out]
        cols = pl.ds((j % tiles_per_out) * LANES, LANES)
        for r in range(SUB_DIL):
            t4 = zs[j, pl.ds(r, n4, stride=SUB_DIL), :]
            z4[j, r * n4:(r + 1) * n4, :] = t4
            o4[r, :, cols] = t4.astype(BF16)
        for r in range(SUB_DIL):
            for r2 in range(SUB_DIL):
                t16 = z4[j, pl.ds(r * n4 + r2, n16, stride=SUB_DIL), :]
                o16[SUB_DIL * r + r2, :, cols] = t16.astype(BF16)


def _proj(x, norm, mod, w_in, layer):
    b, s, d = x.shape
    tm = PROJ_TM
    assert tm % (16 * SUB_DIL ** 2) == 0
    nat = [pl.BlockSpec((None, tm, width), lambda bi, i: (bi, i, 0)) for _, width, _, _ in _PROJ_OUTS]
    nat_shapes = [jax.ShapeDtypeStruct((b, s, width), dtype) for _, width, _, dtype in _PROJ_OUTS]
    regrouped, regrouped_shapes = [], []
    for classes in (SUB_DIL, SUB_DIL ** 2):
        for _ in range(N_REGROUPED):
            regrouped.append(pl.BlockSpec((None, classes, tm // classes, WIDTH_A),
                                          lambda bi, i: (bi, 0, i, 0)))
            regrouped_shapes.append(jax.ShapeDtypeStruct((b, classes, s // classes, WIDTH_A), BF16))
    n_tiles = N_REGROUPED * WIDTH_A // LANES
    return pl.pallas_call(
        _proj_kernel,
        grid=(b, s // tm),
        in_specs=[
            pl.BlockSpec((None, tm, d), lambda bi, i: (bi, i, 0)),
            _layer_row(d, layer),
            _mod_spec(d, layer, 1), _mod_spec(d, layer, 0),
            _layer_resident(w_in.shape, layer),
        ],
        out_specs=nat + regrouped,
        out_shape=nat_shapes + regrouped_shapes,
        scratch_shapes=[
            pltpu.VMEM((n_tiles, tm, LANES), F32),
            pltpu.VMEM((n_tiles, tm, LANES), F32),
        ],
        compiler_params=_params(("arbitrary", "arbitrary")),
        name="proj",
    )(x, norm, mod, mod, w_in)


def _stack_heads(qb, is_h0):
    zero = jnp.zeros_like(qb)
    return jnp.concatenate([jnp.where(is_h0, qb, zero), jnp.where(is_h0, zero, qb)], axis=0)


def _unstack(x, is_h0):
    rows = x.shape[0] // 2
    return jnp.where(is_h0, x[:rows], x[rows:])


def _scores(qb, kb, bias, is_h0):
    return lax.dot_general(_stack_heads(qb, is_h0), kb, (((1,), (1,)), ((), ())),
                           preferred_element_type=F32) + bias


def _row_max(s):
    return jnp.broadcast_to(jnp.max(s, axis=-1, keepdims=True), (s.shape[0], LANES))


def _probabilities(s, m):
    return jnp.exp(s - jnp.tile(m, (1, s.shape[1] // LANES))).astype(BF16)


def _pv_and_sum(p, vb):
    ones = jnp.ones_like(vb)
    both = jnp.dot(p, jnp.concatenate([vb, ones], axis=1), preferred_element_type=F32)
    return both[:, :LANES], both[:, LANES:]


def _three_stage_pipeline(n_groups, stage_qk, stage_softmax, stage_pv):
    assert n_groups % 2 == 0 and n_groups >= 2
    stage_qk(0, 0)
    stage_qk(1, 1)
    stage_softmax(0, 0)

    def two_steps(t, carry):
        g = 2 * t + 1
        stage_qk(g + 1, 0)
        stage_softmax(g, 1)
        stage_pv(g - 1, 0)
        stage_qk(g + 2, 1)
        stage_softmax(g + 1, 0)
        stage_pv(g, 1)
        return carry

    lax.fori_loop(0, (n_groups - 2) // 2, two_steps, 0)
    last = n_groups - 1
    stage_softmax(last, 1)
    stage_pv(last - 1, 0)
    stage_pv(last, 1)


def _aligned(start, align):
    return start if isinstance(start, int) else pl.multiple_of(start, align)


def _attn_a_kernel(slopes_ref, q1, k1, v1, q4, k4, v4, q16, k16, v16, o_ref,
                   tab, s_bufs, p_bufs, m_buf, acc_g, max_g, den_g,
                   acc_n, max_n, den_n, *, seq):
    hp = pl.program_id(1)
    lane = lax.broadcasted_iota(jnp.int32, (1, LANES), 1)
    is_h0 = lane < HEAD_DIM
    pad = HALF_WINDOW
    n_blocks = seq // A_TQ

    qi = lax.broadcasted_iota(jnp.int32, (A_TQ, A_BAND), 0)
    kj = lax.broadcasted_iota(jnp.int32, (A_TQ, A_BAND), 1)
    branch_refs = ((q1, k1, v1), (q4, k4, v4), (q16, k16, v16))

    for o, ((_, dil), (q_ref, k_ref, v_ref)) in enumerate(zip(DILATED_CONFIGS, branch_refs)):
        n = seq // dil
        nbk = n // A_TQ
        assert n % A_TQ == 0 and nbk >= 2 and n >= A_BAND

        for hh in range(2):
            slope = slopes_ref[2 * hp + hh]
            head_rows = pl.ds(hh * A_TQ, A_TQ)

            def table(band_lead, keep=None, slope=slope, dil=dil):
                arel = jnp.abs(kj - band_lead - qi)
                ok = arel <= HALF_WINDOW
                if keep is not None:
                    ok = ok & keep
                return jnp.where(ok, -(slope * (arel * dil).astype(F32)), NEG)

            tab[0, head_rows, :] = table(pad)
            tab[1, head_rows, :] = table(pad, kj >= pad)
            tab[2, head_rows, :] = table(pad, kj < pad + A_TQ)
            tab[3, head_rows, :] = table(0)
            tab[4, head_rows, :] = table(2 * pad)

        if dil == 1:
            dst_acc, dst_max, dst_den = acc_n.at[0], max_n.at[0], den_n.at[0]
        else:
            dst_acc, dst_max, dst_den = acc_g, max_g, den_g

        def blocks(g):
            for u in range(A_GROUP):
                jb = g * A_GROUP + u
                row0 = jb * A_TQ
                band0 = jnp.clip(row0 - pad, 0, seq - A_BAND)
                yield u, jb, _aligned(row0, A_TQ), pl.multiple_of(band0, pad)

        def stage_qk(g, slot, nbk=nbk, q_ref=q_ref, k_ref=k_ref):
            for u, jb, row0, band0 in blocks(g):
                local = jb % nbk
                var = jnp.where(jb == 0, 3, jnp.where(jb == n_blocks - 1, 4, jnp.where(
                    local == 0, 1, jnp.where(local == nbk - 1, 2, 0))))
                s_bufs[slot, u] = _scores(q_ref[pl.ds(row0, A_TQ), :],
                                          k_ref[pl.ds(band0, A_BAND), :], tab[var], is_h0)

        def stage_softmax(g, slot, dst_max=dst_max):
            for u in range(A_GROUP):
                m_buf[u] = _row_max(s_bufs[slot, u])
            for u, _, row0, _ in blocks(g):
                m = m_buf[u]
                p_bufs[slot, u] = _probabilities(s_bufs[slot, u], m)
                dst_max[pl.ds(row0, A_TQ), :] = _unstack(m, is_h0)

        def stage_pv(g, slot, dst_acc=dst_acc, dst_den=dst_den, v_ref=v_ref):
            for u, _, row0, band0 in blocks(g):
                pv, den = _pv_and_sum(p_bufs[slot, u], v_ref[pl.ds(band0, A_BAND), :])
                dst_acc[pl.ds(row0, A_TQ), :] = _unstack(pv, is_h0)
                dst_den[pl.ds(row0, A_TQ), :] = _unstack(den, is_h0)

        _three_stage_pipeline(seq // (A_TQ * A_GROUP), stage_qk, stage_softmax, stage_pv)

        if dil > 1:
            rows = min(n, COPY_ROWS)

            def ungroup(c, carry, o=o, n=n, dil=dil, rows=rows):
                first = c if dil == SUB_DIL else SUB_DIL * (c % SUB_DIL) + c // SUB_DIL
                for part in range(n // rows):
                    src = pl.ds(pl.multiple_of(c * n + part * rows, rows), rows)
                    dst = pl.ds(first + part * rows * dil, rows, stride=dil)
                    acc_n[o, dst, :] = acc_g[src, :]
                    max_n[o, dst, :] = max_g[src, :]
                    den_n[o, dst, :] = den_g[src, :]
                return carry

            lax.fori_loop(0, dil, ungroup, 0)

    def merge(c, carry):
        rows = pl.ds(pl.multiple_of(c * COPY_ROWS, COPY_ROWS), COPY_ROWS)
        m0, m1, m2 = max_n[0, rows, :], max_n[1, rows, :], max_n[2, rows, :]
        m = jnp.maximum(jnp.maximum(m0, m1), m2)
        w0, w1, w2 = jnp.exp(m0 - m), jnp.exp(m1 - m), jnp.exp(m2 - m)
        num = w0 * acc_n[0, rows, :] + w1 * acc_n[1, rows, :] + w2 * acc_n[2, rows, :]
        den = w0 * den_n[0, rows, :] + w1 * den_n[1, rows, :] + w2 * den_n[2, rows, :]
        o_ref[rows, :] = num / den
        return carry

    lax.fori_loop(0, seq // COPY_ROWS, merge, 0)


def _attn_a(qkv, slopes):
    b, s, width = qkv[0].shape
    pairs = width // LANES
    blk = pl.BlockSpec((None, s, LANES), lambda bi, hp, *_: (bi, 0, hp))
    n_branch = len(DILATED_CONFIGS)
    assert s % (2 * A_TQ * A_GROUP) == 0 and s % COPY_ROWS == 0
    return pl.pallas_call(
        functools.partial(_attn_a_kernel, seq=s),
        grid_spec=pltpu.PrefetchScalarGridSpec(
            num_scalar_prefetch=1,
            grid=(b, pairs),
            in_specs=[blk] * len(qkv),
            out_specs=blk,
            scratch_shapes=[
                pltpu.VMEM((5, 2 * A_TQ, A_BAND), F32),
                pltpu.VMEM((2, A_GROUP, 2 * A_TQ, A_BAND), F32),
                pltpu.VMEM((2, A_GROUP, 2 * A_TQ, A_BAND), BF16),
                pltpu.VMEM((A_GROUP, 2 * A_TQ, LANES), F32),
                pltpu.VMEM((s, LANES), F32),
                pltpu.VMEM((s, LANES), F32),
                pltpu.VMEM((s, LANES), F32),
                pltpu.VMEM((n_branch, s, LANES), F32),
                pltpu.VMEM((n_branch, s, LANES), F32),
                pltpu.VMEM((n_branch, s, LANES), F32),
            ],
        ),
        out_shape=jax.ShapeDtypeStruct((b, s, width), F32),
        compiler_params=_params(("arbitrary", "arbitrary")),
        name="attn_dilated",
    )(slopes, *qkv)


def _na_variant_rows(n_rows):
    return (NA_ROWS // 2, 0, C_QROWS, n_rows - 2 * C_QROWS, n_rows - C_QROWS)


def _na_bias_tables(rpb, n_rows):
    n_heads, n_dr, n_dc = rpb.shape
    w = GRID_W
    left = w - 1 - (NA_COLS - 1)
    padded = jnp.pad(rpb.astype(F32), ((0, 0), (0, 0), (left, 2 * w - left - n_dc)),
                     constant_values=NEG)
    skew = jnp.broadcast_to(padded[:, :, None, :], (n_heads, n_dr, w, 2 * w))
    skew = skew.reshape(n_heads, n_dr, 2 * w * w)[:, :, :w * (2 * w - 1)]
    toep = skew.reshape(n_heads, n_dr, w, 2 * w - 1)[:, :, :, w - 1:]
    qc = np.arange(w).reshape(-1, 1)
    kc = np.arange(w).reshape(1, -1)
    cstart = np.clip(qc - NA_COLS // 2, 0, w - NA_COLS)
    col_in = (kc >= cstart) & (kc < cstart + NA_COLS)
    toep = jnp.where(col_in, toep, NEG)
    masked = jnp.full((n_heads, w, w), NEG, F32)
    tables = []
    for r in _na_variant_rows(n_rows):
        base = int(np.clip(r - NA_ROWS // 2, 0, n_rows - C_KROWS))
        q_rows = []
        for a in range(C_QROWS):
            row = r + a
            rstart = int(np.clip(row - NA_ROWS // 2, 0, n_rows - NA_ROWS))
            tiles = []
            for i in range(C_KROWS):
                krow = base + i
                valid = rstart <= krow < rstart + NA_ROWS
                tiles.append(toep[:, krow - row + NA_ROWS - 1] if valid else masked)
            q_rows.append(jnp.concatenate(tiles, axis=-1))
        table = jnp.concatenate(q_rows, axis=-2)
        tables.append(table.reshape(n_heads // 2, 2 * C_QROWS * w, C_KROWS * w))
    return jnp.stack(tables, axis=1)


def _attn_c_kernel(q_ref, k_ref, v_ref, bias_ref, o_ref, s_bufs, p_bufs, m_buf, *, n_rows):
    lane = lax.broadcasted_iota(jnp.int32, (1, LANES), 1)
    is_h0 = lane < HEAD_DIM
    n_blocks = n_rows // C_QROWS
    tq = C_QROWS * GRID_W
    tk = C_KROWS * GRID_W
    assert n_blocks >= 5 and n_blocks % (2 * C_GROUP) == 0

    def blocks(g):
        for u in range(C_GROUP):
            p = g * C_GROUP + u
            base = jnp.clip(p * C_QROWS - NA_ROWS // 2, 0, n_rows - C_KROWS)
            k0 = base * GRID_W
            yield u, p, _aligned(p * tq, tq), pl.multiple_of(k0, GRID_W)

    def stage_qk(g, slot):
        for u, p, q0, k0 in blocks(g):
            var = jnp.where(p == 0, 1, jnp.where(p == 1, 2, jnp.where(
                p == n_blocks - 2, 3, jnp.where(p == n_blocks - 1, 4, 0))))
            s_bufs[slot, u] = _scores(q_ref[pl.ds(q0, tq), :], k_ref[pl.ds(k0, tk), :],
                                      bias_ref[var], is_h0)

    def stage_softmax(g, slot):
        for u in range(C_GROUP):
            m_buf[u] = _row_max(s_bufs[slot, u])
        for u in range(C_GROUP):
            p_bufs[slot, u] = _probabilities(s_bufs[slot, u], m_buf[u])

    def stage_pv(g, slot):
        for u, _, q0, k0 in blocks(g):
            pv, den = _pv_and_sum(p_bufs[slot, u], v_ref[pl.ds(k0, tk), :])
            o_ref[pl.ds(q0, tq), :] = _unstack(pv, is_h0) / _unstack(den, is_h0)

    _three_stage_pipeline(n_blocks // C_GROUP, stage_qk, stage_softmax, stage_pv)


def _attn_c(qc, kc, vc, bias):
    b, s, width = qc.shape
    pairs = width // LANES
    tq = C_QROWS * GRID_W
    tk = C_KROWS * GRID_W
    blk = pl.BlockSpec((None, s, LANES), lambda bi, hp: (bi, 0, hp))
    return pl.pallas_call(
        functools.partial(_attn_c_kernel, n_rows=s // GRID_W),
        grid=(b, pairs),
        in_specs=[blk, blk, blk,
                  pl.BlockSpec((None,) + bias.shape[1:], lambda bi, hp: (hp, 0, 0, 0))],
        out_specs=blk,
        out_shape=jax.ShapeDtypeStruct((b, s, width), F32),
        scratch_shapes=[
            pltpu.VMEM((2, C_GROUP, 2 * tq, tk), F32),
            pltpu.VMEM((2, C_GROUP, 2 * tq, tk), BF16),
            pltpu.VMEM((C_GROUP, 2 * tq, LANES), F32),
        ],
        compiler_params=_params(("arbitrary", "arbitrary")),
        name="attn_neighbourhood",
    )(qc, kc, vc, bias)


def _mix_ffn_kernel(oa_ref, oc_ref, u_ref, up_ref, un_ref, x_ref,
                    na_ref, nc_ref, wpool_ref, ps_ref, wout_ref, g1_ref,
                    nf_ref, sc2_ref, sh2_ref, w1_ref, w2_ref, g2_ref, nfin_ref,
                    o_ref, cat, act, ext, *, tm, seq, d_ff, final_norm):
    i = pl.program_id(1)
    halo = POOL_HALO

    u = u_ref[...]
    width_b = u.shape[-1]
    ext[0:halo, :] = jnp.where(i > 0, up_ref[...], 0.0)
    ext[halo:halo + tm, :] = u
    ext[halo + tm:halo + tm + halo, :] = jnp.where(i < pl.num_programs(1) - 1, un_ref[...], 0.0)
    t = i * tm + lax.broadcasted_iota(jnp.int32, (tm, 1), 0)
    lane = lax.broadcasted_iota(jnp.int32, (1, width_b), 1)
    num = jnp.zeros((tm, width_b), F32)
    den = jnp.zeros((tm, width_b), F32)
    acc = jnp.zeros((tm, width_b), F32)
    done = 0
    for g, w in enumerate(POOL_WINDOWS):
        for d in list(range(-(w // 2), -done)) + list(range(done, w // 2)):
            acc = acc + ext[halo + d:halo + d + tm, :]
        done = w // 2
        lo = jnp.clip(t - w // 2, 0, seq - 1)
        hi = jnp.clip(t + w // 2 - 1, 0, seq - 1)
        cnt = (hi - lo + 1).astype(F32)
        in_group = (lane >= g * POOL_GROUP_DIM) & (lane < (g + 1) * POOL_GROUP_DIM)
        num = jnp.where(in_group, acc, num)
        den = jnp.where(in_group, cnt, den)
    pooled = num / den - u
    y = jnp.dot(pooled.astype(BF16), wpool_ref[...], preferred_element_type=F32) * ps_ref[...]

    wa = oa_ref.shape[-1]
    cat[:, 0:wa] = _rms(oa_ref[...], na_ref[...]).astype(BF16)
    cat[:, wa:wa + width_b] = y.astype(BF16)
    cat[:, wa + width_b:] = _rms(oc_ref[...], nc_ref[...]).astype(BF16)
    mix = jnp.dot(cat[...], wout_ref[...], preferred_element_type=F32)
    x1 = x_ref[...] + g1_ref[...] * mix

    h2 = (_rms(x1, nf_ref[...]) * (1.0 + sc2_ref[...]) + sh2_ref[...]).astype(BF16)
    for c in range(d_ff // FFN_CHUNK):
        lo = c * FFN_CHUNK
        gate = jnp.dot(h2, w1_ref[:, lo:lo + FFN_CHUNK], preferred_element_type=F32)
        up = jnp.dot(h2, w1_ref[:, d_ff + lo:d_ff + lo + FFN_CHUNK], preferred_element_type=F32)
        act[:, lo:lo + FFN_CHUNK] = (gate * jax.nn.sigmoid(gate) * up).astype(BF16)
    ffn = jnp.dot(act[...], w2_ref[...], preferred_element_type=F32)
    x2 = x1 + g2_ref[...] * ffn
    if final_norm:
        x2 = _rms(x2, nfin_ref[...])
    o_ref[...] = x2


def _mix_ffn(oa, oc, ub, x, norm_a, norm_c, w_pool_bd, pool_scale, w_out, mod,
             norm_ffn, w1, w2, norm_final, layer, final_norm):
    b, s, d = x.shape
    tm = FFN_TM
    d_ff = w2.shape[1]
    width_a, width_c, width_b = oa.shape[-1], oc.shape[-1], ub.shape[-1]
    assert d_ff % FFN_CHUNK == 0 and tm % POOL_HALO == 0
    hb = tm // POOL_HALO
    n_hb = s // POOL_HALO
    tile = lambda width: pl.BlockSpec((None, tm, width), lambda bi, i: (bi, i, 0))
    return pl.pallas_call(
        functools.partial(_mix_ffn_kernel, tm=tm, seq=s, d_ff=d_ff, final_norm=final_norm),
        grid=(b, s // tm),
        in_specs=[
            tile(width_a), tile(width_c), tile(width_b),
            pl.BlockSpec((None, POOL_HALO, width_b),
                         lambda bi, i: (bi, jnp.maximum(i * hb - 1, 0), 0)),
            pl.BlockSpec((None, POOL_HALO, width_b),
                         lambda bi, i: (bi, jnp.minimum((i + 1) * hb, n_hb - 1), 0)),
            tile(d),
            _layer_row(width_a, layer), _layer_row(width_c, layer),
            _layer_resident(w_pool_bd.shape, layer), _layer_row(width_b, layer),
            _layer_resident(w_out.shape, layer), _mod_spec(d, layer, 2),
            _layer_row(d, layer), _mod_spec(d, layer, 4), _mod_spec(d, layer, 3),
            _layer_resident(w1.shape, layer), _layer_resident(w2.shape, layer),
            _mod_spec(d, layer, 5),
            pl.BlockSpec((1, d), lambda bi, i: (0, 0)),
        ],
        out_specs=tile(d),
        out_shape=jax.ShapeDtypeStruct((b, s, d), F32),
        scratch_shapes=[
            pltpu.VMEM((tm, d), BF16),
            pltpu.VMEM((tm, d_ff), BF16),
            pltpu.VMEM((tm + 2 * POOL_HALO, width_b), F32),
        ],
        compiler_params=_params(("arbitrary", "arbitrary")),
        name="mix_ffn",
    )(oa, oc, ub, ub, ub, x, norm_a, norm_c, w_pool_bd, pool_scale, w_out, mod,
      norm_ffn, mod, mod, w1, w2, mod, norm_final.reshape(1, d))


def _block_diag(w_pool):
    depth, g, gd, _ = w_pool.shape
    out = jnp.zeros((depth, g * gd, g * gd), w_pool.dtype)
    for j in range(g):
        out = out.at[:, j * gd:(j + 1) * gd, j * gd:(j + 1) * gd].set(w_pool[:, j])
    return out


def kernel(x, c, w_ada, b_ada, norm_mix, w_in, norm_a_out, norm_c_out, w_pool, pool_scale, rpb,
           w_out, norm_ffn, w_ffn_in, w_ffn_out, norm_final):
    depth = w_ada.shape[0]
    b, s, d = x.shape
    assert s % GRID_W == 0 and s % PROJ_TM == 0 and s % FFN_TM == 0

    mod = _ada(c, w_ada, b_ada).reshape(depth, b, N_MOD, 1, d)
    heads = jnp.arange(N_HEADS_A, dtype=F32)
    slopes = 2.0 ** (-8.0 * (heads + 1.0) / N_HEADS_A)

    per_channel = lambda p: p.reshape(depth, 1, p.shape[-1])
    w_in_b, w_out_b = w_in.astype(BF16), w_out.astype(BF16)
    w1_b, w2_b = w_ffn_in.astype(BF16), w_ffn_out.astype(BF16)
    w_pool_b = _block_diag(w_pool).astype(BF16)
    norm_mix_r, norm_a_r, norm_c_r = per_channel(norm_mix), per_channel(norm_a_out), per_channel(norm_c_out)
    pool_scale_r, norm_ffn_r = per_channel(pool_scale), per_channel(norm_ffn)

    for l in range(depth):
        proj = _proj(x, norm_mix_r, mod, w_in_b, l)
        ub, qc, kc, vc = proj[N_REGROUPED:len(_PROJ_OUTS)]
        qkv = proj[:N_REGROUPED] + proj[len(_PROJ_OUTS):]
        oa = _attn_a([t.reshape(b, s, WIDTH_A) for t in qkv], slopes)
        oc = _attn_c(qc, kc, vc, _na_bias_tables(rpb[l], s // GRID_W))
        x = _mix_ffn(oa, oc, ub, x, norm_a_r, norm_c_r, w_pool_b, pool_scale_r, w_out_b, mod,
                     norm_ffn_r, w1_b, w2_b, norm_final, l, final_norm=(l == depth - 1))
    return x
```

```python
import functools

import numpy as np
import jax
import jax.numpy as jnp
from jax import lax
from jax.experimental import pallas as pl
from jax.experimental.pallas import tpu as pltpu

F32 = jnp.float32
BF16 = jnp.bfloat16

HEAD_DIM = 64
N_HEADS_A = 6
N_HEADS_C = 6
WIDTH_A = N_HEADS_A * HEAD_DIM
WIDTH_C = N_HEADS_C * HEAD_DIM
POOL_WINDOWS = (2, 4, 8, 16)
POOL_GROUP_DIM = 64
WIDTH_B = len(POOL_WINDOWS) * POOL_GROUP_DIM
DILATED_CONFIGS = ((128, 1), (512, 4), (2048, 16))
HALF_WINDOW = 64
GRID_W = 64
NA_ROWS = 8
NA_COLS = 16
N_MOD = 6
EPS = 1e-6
NEG = -1e30
LOG2E = 1.4426950408889634
Q_SCALE = HEAD_DIM ** -0.5 * LOG2E

LANES = 128
VMEM_LIMIT_BYTES = 56 * 1024 * 1024

PROJ_TM = 512
PROJ_CHUNK = 512
FFN_TM = 512
FFN_CHUNK = 256
FFN_PARTS = 2
A_TQ = 128
A_BAND = A_TQ + 2 * HALF_WINDOW
A_GROUP = 4
C_QROWS = 2
C_KROWS = 10
C_GROUP = 2
POOL_HALO = 8
COPY_ROWS = 512


def _params(semantics):
    return pltpu.CompilerParams(dimension_semantics=semantics,
                                vmem_limit_bytes=VMEM_LIMIT_BYTES)


def _layer_resident(shape, layer):
    zeros = (0,) * (len(shape) - 1)
    return pl.BlockSpec((None,) + tuple(shape[1:]), lambda *_: (layer,) + zeros,
                        pipeline_mode=pl.Buffered(1))


def _layer_row(width, layer):
    return pl.BlockSpec((None, 1, width), lambda *_: (layer, 0, 0))


def _mod_spec(d, layer, j):
    return pl.BlockSpec((None, None, None, 1, d), lambda bi, i: (layer, bi, j, 0, 0))


def _rms(x, g):
    ms = jnp.mean(x * x, axis=-1, keepdims=True)
    return (x * lax.rsqrt(ms + EPS)) * g


def _ada_kernel(c_ref, w_ref, b_ref, o_ref):
    c = c_ref[...]
    act = c * jax.nn.sigmoid(c)
    o_ref[...] = jnp.dot(act, w_ref[...], preferred_element_type=F32,
                         precision=lax.Precision.HIGHEST) + b_ref[...]


def _ada(c, w_ada, b_ada):
    depth, d, six_d = w_ada.shape
    b = c.shape[0]
    n_chunks = six_d // d
    return pl.pallas_call(
        _ada_kernel,
        grid=(depth, n_chunks),
        in_specs=[
            pl.BlockSpec((b, d), lambda l, j: (0, 0)),
            pl.BlockSpec((None, d, d), lambda l, j: (l, 0, j)),
            pl.BlockSpec((None, 1, d), lambda l, j: (l, 0, j)),
        ],
        out_specs=pl.BlockSpec((None, b, d), lambda l, j: (l, 0, j)),
        out_shape=jax.ShapeDtypeStruct((depth, b, six_d), F32),
        compiler_params=_params(("arbitrary", "arbitrary")),
        name="ada",
    )(c, w_ada, b_ada.reshape(depth, 1, six_d))


_PROJ_OUTS = (
    ("qa", WIDTH_A, Q_SCALE, BF16),
    ("ka", WIDTH_A, None, BF16),
    ("va", WIDTH_A, None, BF16),
    ("ub", WIDTH_B, None, F32),
    ("qc", WIDTH_C, Q_SCALE, BF16),
    ("kc", WIDTH_C, None, BF16),
    ("vc", WIDTH_C, None, BF16),
)
N_REGROUPED = 3
SUB_DIL = 4
assert tuple(dil for _, dil in DILATED_CONFIGS) == (1, SUB_DIL, SUB_DIL ** 2)


def _proj_kernel(x_ref, g_ref, sc_ref, sh_ref, w_ref, *refs):
    n_out = len(_PROJ_OUTS)
    out_refs = refs[:n_out]
    o4_refs = refs[n_out:n_out + N_REGROUPED]
    o16_refs = refs[n_out + N_REGROUPED:n_out + 2 * N_REGROUPED]
    zs, z4 = refs[n_out + 2 * N_REGROUPED:]
    tm = x_ref.shape[0]

    h = _rms(x_ref[...], g_ref[...]) * (1.0 + sc_ref[...]) + sh_ref[...]
    h = h.astype(BF16)
    total = w_ref.shape[-1]
    for c0 in range(0, total, PROJ_CHUNK):
        c1 = min(c0 + PROJ_CHUNK, total)
        z = jnp.dot(h, w_ref[:, c0:c1], preferred_element_type=F32)
        lo = 0
        for idx, ((_, width, scale, dtype), o_ref) in enumerate(zip(_PROJ_OUTS, out_refs)):
            a, b = max(lo, c0), min(lo + width, c1)
            if a < b:
                piece = z[:, a - c0:b - c0]
                if scale is not None:
                    piece = piece * scale
                o_ref[:, a - lo:b - lo] = piece.astype(dtype)
                if idx < N_REGROUPED:
                    for col in range(a, b, LANES):
                        zs[col // LANES] = piece[:, col - a:col - a + LANES]
            lo += width

    n4, n16 = tm // SUB_DIL, tm // SUB_DIL ** 2
    tiles_per_out = WIDTH_A // LANES
    for j in range(N_REGROUPED * tiles_per_out):
        o4, o16 = o4_refs[j // tiles_per_out], o16_refs[j // tiles_per_out]
        cols = pl.ds((j % tiles_per_out) * LANES, LANES)
        for r in range(SUB_DIL):
            t4 = zs[j, pl.ds(r, n4, stride=SUB_DIL), :]
            z4[j, r * n4:(r + 1) * n4, :] = t4
            o4[r, :, cols] = t4.astype(BF16)
        for r in range(SUB_DIL):
            for r2 in range(SUB_DIL):
                t16 = z4[j, pl.ds(r * n4 + r2, n16, stride=SUB_DIL), :]
                o16[SUB_DIL * r + r2, :, cols] = t16.astype(BF16)


def _proj(x, norm, mod, w_in, layer):
    b, s, d = x.shape
    tm = PROJ_TM
    assert tm % (16 * SUB_DIL ** 2) == 0
    nat = [pl.BlockSpec((None, tm, width), lambda bi, i: (bi, i, 0)) for _, width, _, _ in _PROJ_OUTS]
    nat_shapes = [jax.ShapeDtypeStruct((b, s, width), dtype) for _, width, _, dtype in _PROJ_OUTS]
    regrouped, regrouped_shapes = [], []
    for classes in (SUB_DIL, SUB_DIL ** 2):
        for _ in range(N_REGROUPED):
            regrouped.append(pl.BlockSpec((None, classes, tm // classes, WIDTH_A),
                                          lambda bi, i: (bi, 0, i, 0)))
            regrouped_shapes.append(jax.ShapeDtypeStruct((b, classes, s // classes, WIDTH_A), BF16))
    n_tiles = N_REGROUPED * WIDTH_A // LANES
    return pl.pallas_call(
        _proj_kernel,
        grid=(b, s // tm),
        in_specs=[
            pl.BlockSpec((None, tm, d), lambda bi, i: (bi, i, 0)),
            _layer_row(d, layer),
            _mod_spec(d, layer, 1), _mod_spec(d, layer, 0),
            _layer_resident(w_in.shape, layer),
        ],
        out_specs=nat + regrouped,
        out_shape=nat_shapes + regrouped_shapes,
        scratch_shapes=[
            pltpu.VMEM((n_tiles, tm, LANES), F32),
            pltpu.VMEM((n_tiles, tm, LANES), F32),
        ],
        compiler_params=_params(("arbitrary", "arbitrary")),
        name="proj",
    )(x, norm, mod, mod, w_in)


def _stack_heads(qb, is_h0):
    zero = jnp.zeros_like(qb)
    return jnp.concatenate([jnp.where(is_h0, qb, zero), jnp.where(is_h0, zero, qb)], axis=0)


def _unstack(x, is_h0):
    rows = x.shape[0] // 2
    return jnp.where(is_h0, x[:rows], x[rows:])


def _scores(qb, kb, bias, is_h0):
    return lax.dot_general(_stack_heads(qb, is_h0), kb, (((1,), (1,)), ((), ())),
                           preferred_element_type=F32) + bias


def _row_max(s):
    return jnp.broadcast_to(jnp.max(s, axis=-1, keepdims=True), (s.shape[0], LANES))


def _probabilities(s, m):
    return jnp.exp2(s - jnp.tile(m, (1, s.shape[1] // LANES))).astype(BF16)


def _pv_and_sum(p, vb):
    ones = jnp.ones_like(vb)
    both = jnp.dot(p, jnp.concatenate([vb, ones], axis=1), preferred_element_type=F32)
    return both[:, :LANES], both[:, LANES:]


def _three_stage_pipeline(n_groups, stage_qk, stage_softmax, stage_pv):
    assert n_groups % 2 == 0 and n_groups >= 2
    stage_qk(0, 0)
    stage_qk(1, 1)
    stage_softmax(0, 0)

    def two_steps(t, carry):
        g = 2 * t + 1
        stage_qk(g + 1, 0)
        stage_softmax(g, 1)
        stage_pv(g - 1, 0)
        stage_qk(g + 2, 1)
        stage_softmax(g + 1, 0)
        stage_pv(g, 1)
        return carry

    lax.fori_loop(0, (n_groups - 2) // 2, two_steps, 0)
    last = n_groups - 1
    stage_softmax(last, 1)
    stage_pv(last - 1, 0)
    stage_pv(last, 1)


def _aligned(start, align):
    return start if isinstance(start, int) else pl.multiple_of(start, align)


def _attn_a_kernel(slopes_ref, q1, k1, v1, q4, k4, v4, q16, k16, v16, o_ref,
                   tab, s_bufs, p_bufs, m_buf, acc_g, max_g, den_g,
                   acc_n, max_n, den_n, *, seq):
    hp = pl.program_id(0)
    lane = lax.broadcasted_iota(jnp.int32, (1, LANES), 1)
    is_h0 = lane < HEAD_DIM
    pad = HALF_WINDOW
    n_blocks = seq // A_TQ

    @pl.when(pl.program_id(1) == 0)
    def _build_tables():
        qi = lax.broadcasted_iota(jnp.int32, (A_TQ, A_BAND), 0)
        kj = lax.broadcasted_iota(jnp.int32, (A_TQ, A_BAND), 1)
        for o, (_, dil) in enumerate(DILATED_CONFIGS):
            for hh in range(2):
                slope = slopes_ref[2 * hp + hh]
                head_rows = pl.ds(hh * A_TQ, A_TQ)

                def table(band_lead, keep=None, slope=slope, dil=dil):
                    arel = jnp.abs(kj - band_lead - qi)
                    ok = arel <= HALF_WINDOW
                    if keep is not None:
                        ok = ok & keep
                    return jnp.where(ok, -(slope * (arel * dil).astype(F32)) * LOG2E, NEG)

                tab[o, 0, head_rows, :] = table(pad)
                tab[o, 1, head_rows, :] = table(pad, kj >= pad)
                tab[o, 2, head_rows, :] = table(pad, kj < pad + A_TQ)
                tab[o, 3, head_rows, :] = table(0)
                tab[o, 4, head_rows, :] = table(2 * pad)

    branch_refs = ((q1, k1, v1), (q4, k4, v4), (q16, k16, v16))

    for o, ((_, dil), (q_ref, k_ref, v_ref)) in enumerate(zip(DILATED_CONFIGS, branch_refs)):
        n = seq // dil
        nbk = n // A_TQ
        assert n % A_TQ == 0 and nbk >= 2 and n >= A_BAND

        if dil == SUB_DIL:
            dst_acc, dst_max, dst_den = acc_n.at[o], max_n.at[o], den_n.at[o]
        else:
            dst_acc, dst_max, dst_den = acc_g, max_g, den_g

        def blocks(g):
            for u in range(A_GROUP):
                jb = g * A_GROUP + u
                row0 = jb * A_TQ
                band0 = jnp.clip(row0 - pad, 0, seq - A_BAND)
                yield u, jb, _aligned(row0, A_TQ), pl.multiple_of(band0, pad)

        def stage_qk(g, slot, o=o, nbk=nbk, q_ref=q_ref, k_ref=k_ref):
            for u, jb, row0, band0 in blocks(g):
                local = jb % nbk
                var = jnp.where(jb == 0, 3, jnp.where(jb == n_blocks - 1, 4, jnp.where(
                    local == 0, 1, jnp.where(local == nbk - 1, 2, 0))))
                s_bufs[slot, u] = _scores(q_ref[pl.ds(row0, A_TQ), :],
                                          k_ref[pl.ds(band0, A_BAND), :], tab[o, var], is_h0)

        def stage_softmax(g, slot, dst_max=dst_max):
            for u in range(A_GROUP):
                m_buf[u] = _row_max(s_bufs[slot, u])
            for u, _, row0, _ in blocks(g):
                m = m_buf[u]
                p_bufs[slot, u] = _probabilities(s_bufs[slot, u], m)
                dst_max[pl.ds(row0, A_TQ), :] = _unstack(m, is_h0)

        def stage_pv(g, slot, dst_acc=dst_acc, dst_den=dst_den, v_ref=v_ref):
            for u, _, row0, band0 in blocks(g):
                pv, den = _pv_and_sum(p_bufs[slot, u], v_ref[pl.ds(band0, A_BAND), :])
                dst_acc[pl.ds(row0, A_TQ), :] = _unstack(pv, is_h0)
                dst_den[pl.ds(row0, A_TQ), :] = _unstack(den, is_h0)

        _three_stage_pipeline(seq // (A_TQ * A_GROUP), stage_qk, stage_softmax, stage_pv)

        n_sub = seq // SUB_DIL
        if dil == 1:
            rows = min(n_sub, COPY_ROWS)

            def to_merge_order(r, carry, o=o, rows=rows):
                for part in range(n_sub // rows):
                    src = pl.ds(r + part * rows * SUB_DIL, rows, stride=SUB_DIL)
                    dst = pl.ds(pl.multiple_of(r * n_sub + part * rows, rows), rows)
                    acc_n[o, dst, :] = acc_g[src, :]
                    max_n[o, dst, :] = max_g[src, :]
                    den_n[o, dst, :] = den_g[src, :]
                return carry

            lax.fori_loop(0, SUB_DIL, to_merge_order, 0)
        elif dil == SUB_DIL ** 2:
            def to_merge_order(c, carry, o=o, n=n):
                src = pl.ds(pl.multiple_of(c * n, n), n)
                dst = pl.ds((c // SUB_DIL) * n_sub + c % SUB_DIL, n, stride=SUB_DIL)
                acc_n[o, dst, :] = acc_g[src, :]
                max_n[o, dst, :] = max_g[src, :]
                den_n[o, dst, :] = den_g[src, :]
                return carry

            lax.fori_loop(0, dil, to_merge_order, 0)

    n_sub = seq // SUB_DIL
    chunks_per_class = n_sub // COPY_ROWS

    def merge(c, carry):
        rows = pl.ds(pl.multiple_of(c * COPY_ROWS, COPY_ROWS), COPY_ROWS)
        m0, m1, m2 = max_n[0, rows, :], max_n[1, rows, :], max_n[2, rows, :]
        m = jnp.maximum(jnp.maximum(m0, m1), m2)
        w0, w1, w2 = jnp.exp2(m0 - m), jnp.exp2(m1 - m), jnp.exp2(m2 - m)
        num = w0 * acc_n[0, rows, :] + w1 * acc_n[1, rows, :] + w2 * acc_n[2, rows, :]
        den = w0 * den_n[0, rows, :] + w1 * den_n[1, rows, :] + w2 * den_n[2, rows, :]
        r, i0 = c // chunks_per_class, (c % chunks_per_class) * COPY_ROWS
        o_ref[pl.ds(r + SUB_DIL * i0, COPY_ROWS, stride=SUB_DIL), :] = num / den
        return carry

    lax.fori_loop(0, seq // COPY_ROWS, merge, 0)


def _attn_a(qkv, slopes):
    b, s, width = qkv[0].shape
    pairs = width // LANES
    blk = pl.BlockSpec((None, s, LANES), lambda hp, bi, *_: (bi, 0, hp))
    n_branch = len(DILATED_CONFIGS)
    assert s % (2 * A_TQ * A_GROUP) == 0 and s % (SUB_DIL * COPY_ROWS) == 0
    return pl.pallas_call(
        functools.partial(_attn_a_kernel, seq=s),
        grid_spec=pltpu.PrefetchScalarGridSpec(
            num_scalar_prefetch=1,
            grid=(pairs, b),
            in_specs=[blk] * len(qkv),
            out_specs=blk,
            scratch_shapes=[
                pltpu.VMEM((n_branch, 5, 2 * A_TQ, A_BAND), F32),
                pltpu.VMEM((2, A_GROUP, 2 * A_TQ, A_BAND), F32),
                pltpu.VMEM((2, A_GROUP, 2 * A_TQ, A_BAND), BF16),
                pltpu.VMEM((A_GROUP, 2 * A_TQ, LANES), F32),
                pltpu.VMEM((s, LANES), F32),
                pltpu.VMEM((s, LANES), F32),
                pltpu.VMEM((s, LANES), F32),
                pltpu.VMEM((n_branch, s, LANES), F32),
                pltpu.VMEM((n_branch, s, LANES), F32),
                pltpu.VMEM((n_branch, s, LANES), F32),
            ],
        ),
        out_shape=jax.ShapeDtypeStruct((b, s, width), F32),
        compiler_params=_params(("arbitrary", "arbitrary")),
        name="attn_dilated",
    )(slopes, *qkv)


def _na_variant_rows(n_rows):
    return (NA_ROWS // 2, 0, C_QROWS, n_rows - 2 * C_QROWS, n_rows - C_QROWS)


def _na_bias_tables(rpb, n_rows):
    n_heads, n_dr, n_dc = rpb.shape
    w = GRID_W
    left = w - 1 - (NA_COLS - 1)
    padded = jnp.pad(rpb.astype(F32) * LOG2E, ((0, 0), (0, 0), (left, 2 * w - left - n_dc)),
                     constant_values=NEG)
    skew = jnp.broadcast_to(padded[:, :, None, :], (n_heads, n_dr, w, 2 * w))
    skew = skew.reshape(n_heads, n_dr, 2 * w * w)[:, :, :w * (2 * w - 1)]
    toep = skew.reshape(n_heads, n_dr, w, 2 * w - 1)[:, :, :, w - 1:]
    qc = np.arange(w).reshape(-1, 1)
    kc = np.arange(w).reshape(1, -1)
    cstart = np.clip(qc - NA_COLS // 2, 0, w - NA_COLS)
    col_in = (kc >= cstart) & (kc < cstart + NA_COLS)
    toep = jnp.where(col_in, toep, NEG)
    masked = jnp.full((n_heads, w, w), NEG, F32)
    tables = []
    for r in _na_variant_rows(n_rows):
        base = int(np.clip(r - NA_ROWS // 2, 0, n_rows - C_KROWS))
        q_rows = []
        for a in range(C_QROWS):
            row = r + a
            rstart = int(np.clip(row - NA_ROWS // 2, 0, n_rows - NA_ROWS))
            tiles = []
            for i in range(C_KROWS):
                krow = base + i
                valid = rstart <= krow < rstart + NA_ROWS
                tiles.append(toep[:, krow - row + NA_ROWS - 1] if valid else masked)
            q_rows.append(jnp.concatenate(tiles, axis=-1))
        table = jnp.concatenate(q_rows, axis=-2)
        tables.append(table.reshape(n_heads // 2, 2 * C_QROWS * w, C_KROWS * w))
    return jnp.stack(tables, axis=1)


def _attn_c_kernel(q_ref, k_ref, v_ref, bias_ref, o_ref, s_bufs, p_bufs, m_buf, *, n_rows):
    lane = lax.broadcasted_iota(jnp.int32, (1, LANES), 1)
    is_h0 = lane < HEAD_DIM
    n_blocks = n_rows // C_QROWS
    tq = C_QROWS * GRID_W
    tk = C_KROWS * GRID_W
    assert n_blocks >= 5 and n_blocks % (2 * C_GROUP) == 0

    def blocks(g):
        for u in range(C_GROUP):
            p = g * C_GROUP + u
            base = jnp.clip(p * C_QROWS - NA_ROWS // 2, 0, n_rows - C_KROWS)
            k0 = base * GRID_W
            yield u, p, _aligned(p * tq, tq), pl.multiple_of(k0, GRID_W)

    def stage_qk(g, slot):
        for u, p, q0, k0 in blocks(g):
            var = jnp.where(p == 0, 1, jnp.where(p == 1, 2, jnp.where(
                p == n_blocks - 2, 3, jnp.where(p == n_blocks - 1, 4, 0))))
            s_bufs[slot, u] = _scores(q_ref[pl.ds(q0, tq), :], k_ref[pl.ds(k0, tk), :],
                                      bias_ref[var], is_h0)

    def stage_softmax(g, slot):
        for u in range(C_GROUP):
            m_buf[u] = _row_max(s_bufs[slot, u])
        for u in range(C_GROUP):
            p_bufs[slot, u] = _probabilities(s_bufs[slot, u], m_buf[u])

    def stage_pv(g, slot):
        for u, _, q0, k0 in blocks(g):
            pv, den = _pv_and_sum(p_bufs[slot, u], v_ref[pl.ds(k0, tk), :])
            o_ref[pl.ds(q0, tq), :] = _unstack(pv, is_h0) / _unstack(den, is_h0)

    _three_stage_pipeline(n_blocks // C_GROUP, stage_qk, stage_softmax, stage_pv)


def _attn_c(qc, kc, vc, bias):
    b, s, width = qc.shape
    pairs = width // LANES
    tq = C_QROWS * GRID_W
    tk = C_KROWS * GRID_W
    blk = pl.BlockSpec((None, s, LANES), lambda bi, hp: (bi, 0, hp))
    return pl.pallas_call(
        functools.partial(_attn_c_kernel, n_rows=s // GRID_W),
        grid=(b, pairs),
        in_specs=[blk, blk, blk,
                  pl.BlockSpec((None,) + bias.shape[1:], lambda bi, hp: (hp, 0, 0, 0))],
        out_specs=blk,
        out_shape=jax.ShapeDtypeStruct((b, s, width), F32),
        scratch_shapes=[
            pltpu.VMEM((2, C_GROUP, 2 * tq, tk), F32),
            pltpu.VMEM((2, C_GROUP, 2 * tq, tk), BF16),
            pltpu.VMEM((C_GROUP, 2 * tq, LANES), F32),
        ],
        compiler_params=_params(("arbitrary", "arbitrary")),
        name="attn_neighbourhood",
    )(qc, kc, vc, bias)


def _mix_ffn_kernel(oa_ref, oc_ref, u_ref, up_ref, un_ref, x_ref,
                    na_ref, nc_ref, wpool_ref, ps_ref, wout_ref, g1_ref,
                    nf_ref, sc2_ref, sh2_ref, w1_ref, w2_ref, g2_ref, nfin_ref,
                    o_ref, cat, act, ext, *, tm, seq, d_ff, final_norm):
    i = pl.program_id(1)
    halo = POOL_HALO

    width_b = u_ref.shape[-1]
    ext[0:halo, :] = jnp.where(i > 0, up_ref[...], 0.0)
    ext[halo:halo + tm, :] = u_ref[...]
    ext[halo + tm:halo + tm + halo, :] = jnp.where(i < pl.num_programs(1) - 1, un_ref[...], 0.0)
    lane = lax.broadcasted_iota(jnp.int32, (1, width_b), 1)
    wa = oa_ref.shape[-1]

    part = tm // FFN_PARTS
    starts = list(range(0, tm, part))
    mixed = {}

    def mixer(r0):
        rows = pl.ds(r0, part)
        t = i * tm + r0 + lax.broadcasted_iota(jnp.int32, (part, 1), 0)
        num = jnp.zeros((part, width_b), F32)
        den = jnp.zeros((part, width_b), F32)
        acc = jnp.zeros((part, width_b), F32)
        done = 0
        for g, w in enumerate(POOL_WINDOWS):
            for d in list(range(-(w // 2), -done)) + list(range(done, w // 2)):
                acc = acc + ext[halo + r0 + d:halo + r0 + d + part, :]
            done = w // 2
            lo = jnp.clip(t - w // 2, 0, seq - 1)
            hi = jnp.clip(t + w // 2 - 1, 0, seq - 1)
            cnt = (hi - lo + 1).astype(F32)
            in_group = (lane >= g * POOL_GROUP_DIM) & (lane < (g + 1) * POOL_GROUP_DIM)
            num = jnp.where(in_group, acc, num)
            den = jnp.where(in_group, cnt, den)
            yield
        pooled = num / den - u_ref[rows, :]
        y = jnp.dot(pooled.astype(BF16), wpool_ref[...], preferred_element_type=F32) * ps_ref[...]
        cat[rows, wa:wa + width_b] = y.astype(BF16)
        yield
        cat[rows, 0:wa] = _rms(oa_ref[rows, :], na_ref[...]).astype(BF16)
        yield
        cat[rows, wa + width_b:] = _rms(oc_ref[rows, :], nc_ref[...]).astype(BF16)
        yield
        mix = jnp.dot(cat[rows, :], wout_ref[...], preferred_element_type=F32)
        x1 = x_ref[rows, :] + g1_ref[...] * mix
        yield
        h2 = (_rms(x1, nf_ref[...]) * (1.0 + sc2_ref[...]) + sh2_ref[...]).astype(BF16)
        mixed[r0] = (x1, h2)

    def swiglu(r0, background):
        rows = pl.ds(r0, part)
        x1, h2 = mixed.pop(r0)
        for c in range(d_ff // FFN_CHUNK):
            lo = c * FFN_CHUNK
            gate = jnp.dot(h2, w1_ref[:, lo:lo + FFN_CHUNK], preferred_element_type=F32)
            up = jnp.dot(h2, w1_ref[:, d_ff + lo:d_ff + lo + FFN_CHUNK],
                         preferred_element_type=F32)
            act[rows, lo:lo + FFN_CHUNK] = (gate * jax.nn.sigmoid(gate) * up).astype(BF16)
            next(background, None)
        for _ in background:
            pass
        ffn = jnp.dot(act[rows, :], w2_ref[...], preferred_element_type=F32)
        x2 = x1 + g2_ref[...] * ffn
        if final_norm:
            x2 = _rms(x2, nfin_ref[...])
        o_ref[rows, :] = x2

    pieces = {r0: mixer(r0) for r0 in starts}
    for _ in pieces[starts[0]]:
        pass
    for k, r0 in enumerate(starts):
        swiglu(r0, pieces[starts[k + 1]] if k + 1 < len(starts) else iter(()))


def _mix_ffn(oa, oc, ub, x, norm_a, norm_c, w_pool_bd, pool_scale, w_out, mod,
             norm_ffn, w1, w2, norm_final, layer, final_norm):
    b, s, d = x.shape
    tm = FFN_TM
    d_ff = w2.shape[1]
    width_a, width_c, width_b = oa.shape[-1], oc.shape[-1], ub.shape[-1]
    assert d_ff % FFN_CHUNK == 0 and tm % POOL_HALO == 0
    hb = tm // POOL_HALO
    n_hb = s // POOL_HALO
    tile = lambda width: pl.BlockSpec((None, tm, width), lambda bi, i: (bi, i, 0))
    return pl.pallas_call(
        functools.partial(_mix_ffn_kernel, tm=tm, seq=s, d_ff=d_ff, final_norm=final_norm),
        grid=(b, s // tm),
        in_specs=[
            tile(width_a), tile(width_c), tile(width_b),
            pl.BlockSpec((None, POOL_HALO, width_b),
                         lambda bi, i: (bi, jnp.maximum(i * hb - 1, 0), 0)),
            pl.BlockSpec((None, POOL_HALO, width_b),
                         lambda bi, i: (bi, jnp.minimum((i + 1) * hb, n_hb - 1), 0)),
            tile(d),
            _layer_row(width_a, layer), _layer_row(width_c, layer),
            _layer_resident(w_pool_bd.shape, layer), _layer_row(width_b, layer),
            _layer_resident(w_out.shape, layer), _mod_spec(d, layer, 2),
            _layer_row(d, layer), _mod_spec(d, layer, 4), _mod_spec(d, layer, 3),
            _layer_resident(w1.shape, layer), _layer_resident(w2.shape, layer),
            _mod_spec(d, layer, 5),
            pl.BlockSpec((1, d), lambda bi, i: (0, 0)),
        ],
        out_specs=tile(d),
        out_shape=jax.ShapeDtypeStruct((b, s, d), F32),
        scratch_shapes=[
            pltpu.VMEM((tm, d), BF16),
            pltpu.VMEM((tm, d_ff), BF16),
            pltpu.VMEM((tm + 2 * POOL_HALO, width_b), F32),
        ],
        compiler_params=_params(("arbitrary", "arbitrary")),
        name="mix_ffn",
    )(oa, oc, ub, ub, ub, x, norm_a, norm_c, w_pool_bd, pool_scale, w_out, mod,
      norm_ffn, mod, mod, w1, w2, mod, norm_final.reshape(1, d))


def _block_diag(w_pool):
    depth, g, gd, _ = w_pool.shape
    out = jnp.zeros((depth, g * gd, g * gd), w_pool.dtype)
    for j in range(g):
        out = out.at[:, j * gd:(j + 1) * gd, j * gd:(j + 1) * gd].set(w_pool[:, j])
    return out


def kernel(x, c, w_ada, b_ada, norm_mix, w_in, norm_a_out, norm_c_out, w_pool, pool_scale, rpb,
           w_out, norm_ffn, w_ffn_in, w_ffn_out, norm_final):
    depth = w_ada.shape[0]
    b, s, d = x.shape
    assert s % GRID_W == 0 and s % PROJ_TM == 0 and s % FFN_TM == 0

    mod = _ada(c, w_ada, b_ada).reshape(depth, b, N_MOD, 1, d)
    heads = jnp.arange(N_HEADS_A, dtype=F32)
    slopes = 2.0 ** (-8.0 * (heads + 1.0) / N_HEADS_A)

    per_channel = lambda p: p.reshape(depth, 1, p.shape[-1])
    w_in_b, w_out_b = w_in.astype(BF16), w_out.astype(BF16)
    w1_b, w2_b = w_ffn_in.astype(BF16), w_ffn_out.astype(BF16)
    w_pool_b = _block_diag(w_pool).astype(BF16)
    norm_mix_r, norm_a_r, norm_c_r = per_channel(norm_mix), per_channel(norm_a_out), per_channel(norm_c_out)
    pool_scale_r, norm_ffn_r = per_channel(pool_scale), per_channel(norm_ffn)

    for l in range(depth):
        proj = _proj(x, norm_mix_r, mod, w_in_b, l)
        ub, qc, kc, vc = proj[N_REGROUPED:len(_PROJ_OUTS)]
        qkv = proj[:N_REGROUPED] + proj[len(_PROJ_OUTS):]
        oa = _attn_a([t.reshape(b, s, WIDTH_A) for t in qkv], slopes)
        oc = _attn_c(qc, kc, vc, _na_bias_tables(rpb[l], s // GRID_W))
        x = _mix_ffn(oa, oc, ub, x, norm_a_r, norm_c_r, w_pool_b, pool_scale_r, w_out_b, mod,
                     norm_ffn_r, w1_b, w2_b, norm_final, l, final_norm=(l == depth - 1))
    return x
```

```python
import functools

import numpy as np
import jax
import jax.numpy as jnp
from jax import lax
from jax.experimental import pallas as pl
from jax.experimental.pallas import tpu as pltpu

F32 = jnp.float32
BF16 = jnp.bfloat16

HEAD_DIM = 64
N_HEADS_A = 6
N_HEADS_C = 6
WIDTH_A = N_HEADS_A * HEAD_DIM
WIDTH_C = N_HEADS_C * HEAD_DIM
POOL_WINDOWS = (2, 4, 8, 16)
POOL_GROUP_DIM = 64
WIDTH_B = len(POOL_WINDOWS) * POOL_GROUP_DIM
DILATED_CONFIGS = ((128, 1), (512, 4), (2048, 16))
HALF_WINDOW = 64
GRID_W = 64
NA_ROWS = 8
NA_COLS = 16
N_MOD = 6
EPS = 1e-6
NEG = -1e30
LOG2E = 1.4426950408889634
Q_SCALE = HEAD_DIM ** -0.5 * LOG2E

LANES = 128
VMEM_LIMIT_BYTES = 60 * 1024 * 1024

PROJ_TM = 512
PROJ_CHUNK = 512
FFN_TM = 512
FFN_CHUNK = 256
FFN_PARTS = 2
A_TQ = 128
A_BAND = A_TQ + 2 * HALF_WINDOW
A_GROUP = 8
C_QROWS = 2
C_KROWS = 10
C_GROUP = 2
POOL_HALO = 8
COPY_ROWS = 512


def _params(semantics):
    return pltpu.CompilerParams(dimension_semantics=semantics,
                                vmem_limit_bytes=VMEM_LIMIT_BYTES)


def _layer_resident(shape, layer):
    zeros = (0,) * (len(shape) - 1)
    return pl.BlockSpec((None,) + tuple(shape[1:]), lambda *_: (layer,) + zeros,
                        pipeline_mode=pl.Buffered(1))


def _layer_row(width, layer):
    return pl.BlockSpec((None, 1, width), lambda *_: (layer, 0, 0))


def _mod_spec(d, layer, j):
    return pl.BlockSpec((None, None, None, 1, d), lambda bi, i: (layer, bi, j, 0, 0))


def _rms(x, g):
    ms = jnp.mean(x * x, axis=-1, keepdims=True)
    return (x * lax.rsqrt(ms + EPS)) * g


def _ada_kernel(c_ref, w_ref, b_ref, o_ref):
    c = c_ref[...]
    act = c * jax.nn.sigmoid(c)
    o_ref[...] = jnp.dot(act, w_ref[...], preferred_element_type=F32,
                         precision=lax.Precision.HIGHEST) + b_ref[...]


def _ada(c, w_ada, b_ada):
    depth, d, six_d = w_ada.shape
    b = c.shape[0]
    n_chunks = six_d // d
    return pl.pallas_call(
        _ada_kernel,
        grid=(depth, n_chunks),
        in_specs=[
            pl.BlockSpec((b, d), lambda l, j: (0, 0)),
            pl.BlockSpec((None, d, d), lambda l, j: (l, 0, j)),
            pl.BlockSpec((None, 1, d), lambda l, j: (l, 0, j)),
        ],
        out_specs=pl.BlockSpec((None, b, d), lambda l, j: (l, 0, j)),
        out_shape=jax.ShapeDtypeStruct((depth, b, six_d), F32),
        compiler_params=_params(("arbitrary", "arbitrary")),
        name="ada",
    )(c, w_ada, b_ada.reshape(depth, 1, six_d))


_PROJ_OUTS = (
    ("qa", WIDTH_A, Q_SCALE, BF16),
    ("ka", WIDTH_A, None, BF16),
    ("va", WIDTH_A, None, BF16),
    ("ub", WIDTH_B, None, F32),
    ("qc", WIDTH_C, Q_SCALE, BF16),
    ("kc", WIDTH_C, None, BF16),
    ("vc", WIDTH_C, None, BF16),
)
N_REGROUPED = 3
SUB_DIL = 4
assert tuple(dil for _, dil in DILATED_CONFIGS) == (1, SUB_DIL, SUB_DIL ** 2)


def _proj_kernel(x_ref, g_ref, sc_ref, sh_ref, w_ref, *refs):
    n_out = len(_PROJ_OUTS)
    out_refs = refs[:n_out]
    o4_refs = refs[n_out:n_out + N_REGROUPED]
    o16_refs = refs[n_out + N_REGROUPED:n_out + 2 * N_REGROUPED]
    zs, z4 = refs[n_out + 2 * N_REGROUPED:]
    tm = x_ref.shape[0]

    h = _rms(x_ref[...], g_ref[...]) * (1.0 + sc_ref[...]) + sh_ref[...]
    h = h.astype(BF16)
    total = w_ref.shape[-1]
    n4, n16 = tm // SUB_DIL, tm // SUB_DIL ** 2
    tiles_per_out = WIDTH_A // LANES

    def regroup(j):
        o4, o16 = o4_refs[j // tiles_per_out], o16_refs[j // tiles_per_out]
        cols = pl.ds((j % tiles_per_out) * LANES, LANES)
        for r in range(SUB_DIL):
            t4 = zs[j, pl.ds(r, n4, stride=SUB_DIL), :]
            z4[j, r * n4:(r + 1) * n4, :] = t4
            o4[r, :, cols] = t4.astype(BF16)
        for r in range(SUB_DIL):
            for r2 in range(SUB_DIL):
                t16 = z4[j, pl.ds(r * n4 + r2, n16, stride=SUB_DIL), :]
                o16[SUB_DIL * r + r2, :, cols] = t16.astype(BF16)

    for c0 in range(0, total, PROJ_CHUNK):
        c1 = min(c0 + PROJ_CHUNK, total)
        z = jnp.dot(h, w_ref[:, c0:c1], preferred_element_type=F32)
        lo = 0
        for idx, ((_, width, scale, dtype), o_ref) in enumerate(zip(_PROJ_OUTS, out_refs)):
            a, b = max(lo, c0), min(lo + width, c1)
            if a < b:
                piece = z[:, a - c0:b - c0]
                if scale is not None:
                    piece = piece * scale
                o_ref[:, a - lo:b - lo] = piece.astype(dtype)
                if idx < N_REGROUPED:
                    for col in range(a, b, LANES):
                        zs[col // LANES] = piece[:, col - a:col - a + LANES]
            lo += width
    for j in range(N_REGROUPED * tiles_per_out):
        regroup(j)


def _proj(x, norm, mod, w_in, layer):
    b, s, d = x.shape
    tm = PROJ_TM
    assert tm % (16 * SUB_DIL ** 2) == 0
    nat = [pl.BlockSpec((None, tm, width), lambda bi, i: (bi, i, 0)) for _, width, _, _ in _PROJ_OUTS]
    nat_shapes = [jax.ShapeDtypeStruct((b, s, width), dtype) for _, width, _, dtype in _PROJ_OUTS]
    regrouped, regrouped_shapes = [], []
    for classes in (SUB_DIL, SUB_DIL ** 2):
        for _ in range(N_REGROUPED):
            regrouped.append(pl.BlockSpec((None, classes, tm // classes, WIDTH_A),
                                          lambda bi, i: (bi, 0, i, 0)))
            regrouped_shapes.append(jax.ShapeDtypeStruct((b, classes, s // classes, WIDTH_A), BF16))
    n_tiles = N_REGROUPED * WIDTH_A // LANES
    return pl.pallas_call(
        _proj_kernel,
        grid=(b, s // tm),
        in_specs=[
            pl.BlockSpec((None, tm, d), lambda bi, i: (bi, i, 0)),
            _layer_row(d, layer),
            _mod_spec(d, layer, 1), _mod_spec(d, layer, 0),
            _layer_resident(w_in.shape, layer),
        ],
        out_specs=nat + regrouped,
        out_shape=nat_shapes + regrouped_shapes,
        scratch_shapes=[
            pltpu.VMEM((n_tiles, tm, LANES), F32),
            pltpu.VMEM((n_tiles, tm, LANES), F32),
        ],
        compiler_params=_params(("arbitrary", "arbitrary")),
        name="proj",
    )(x, norm, mod, mod, w_in)


def _stack_heads(qb, is_h0):
    zero = jnp.zeros_like(qb)
    return jnp.concatenate([jnp.where(is_h0, qb, zero), jnp.where(is_h0, zero, qb)], axis=0)


def _unstack(x, is_h0):
    rows = x.shape[0] // 2
    return jnp.where(is_h0, x[:rows], x[rows:])


def _scores(qb, kb, bias, is_h0):
    return lax.dot_general(_stack_heads(qb, is_h0), kb, (((1,), (1,)), ((), ())),
                           preferred_element_type=F32) + bias


def _row_max(s):
    return jnp.broadcast_to(jnp.max(s, axis=-1, keepdims=True), (s.shape[0], LANES))


def _probabilities(s, m):
    return jnp.exp2(s - jnp.tile(m, (1, s.shape[1] // LANES))).astype(BF16)


def _pv_and_sum(p, vb):
    ones = jnp.ones_like(vb)
    both = jnp.dot(p, jnp.concatenate([vb, ones], axis=1), preferred_element_type=F32)
    return both[:, :LANES], both[:, LANES:]


def _three_stage_pipeline(segments):
    finish = None
    for n_groups, stage_qk, stage_softmax, stage_pv, after in segments:
        assert n_groups % 2 == 0 and n_groups >= 2
        stage_qk(0, 0)
        if finish is not None:
            finish[0]()
        stage_qk(1, 1)
        stage_softmax(0, 0)
        if finish is not None:
            finish[1]()

        def two_steps(t, carry, stage_qk=stage_qk, stage_softmax=stage_softmax, stage_pv=stage_pv):
            g = 2 * t + 1
            stage_qk(g + 1, 0)
            stage_softmax(g, 1)
            stage_pv(g - 1, 0)
            stage_qk(g + 2, 1)
            stage_softmax(g + 1, 0)
            stage_pv(g, 1)
            return carry

        lax.fori_loop(0, (n_groups - 2) // 2, two_steps, 0)
        last = n_groups - 1

        def first_trailing(stage_softmax=stage_softmax, stage_pv=stage_pv, last=last):
            stage_softmax(last, 1)
            stage_pv(last - 1, 0)

        def second_trailing(stage_pv=stage_pv, after=after, last=last):
            stage_pv(last, 1)
            if after is not None:
                after()

        finish = (first_trailing, second_trailing)
    finish[0]()
    finish[1]()


def _aligned(start, align):
    return start if isinstance(start, int) else pl.multiple_of(start, align)


def _attn_a_kernel(slopes_ref, q1, k1, v1, q4, k4, v4, q16, k16, v16, o_ref,
                   tab, s_bufs, p_bufs, m_buf, acc_g, max_g, den_g,
                   acc_n, max_n, den_n, *, seq):
    hp = pl.program_id(0)
    lane = lax.broadcasted_iota(jnp.int32, (1, LANES), 1)
    is_h0 = lane < HEAD_DIM
    pad = HALF_WINDOW
    n_blocks = seq // A_TQ

    @pl.when(pl.program_id(1) == 0)
    def _build_tables():
        qi = lax.broadcasted_iota(jnp.int32, (A_TQ, A_BAND), 0)
        kj = lax.broadcasted_iota(jnp.int32, (A_TQ, A_BAND), 1)
        for o, (_, dil) in enumerate(DILATED_CONFIGS):
            for hh in range(2):
                slope = slopes_ref[2 * hp + hh]
                head_rows = pl.ds(hh * A_TQ, A_TQ)

                def table(band_lead, keep=None, slope=slope, dil=dil):
                    arel = jnp.abs(kj - band_lead - qi)
                    ok = arel <= HALF_WINDOW
                    if keep is not None:
                        ok = ok & keep
                    return jnp.where(ok, -(slope * (arel * dil).astype(F32)) * LOG2E, NEG)

                tab[o, 0, head_rows, :] = table(pad)
                tab[o, 1, head_rows, :] = table(pad, kj >= pad)
                tab[o, 2, head_rows, :] = table(pad, kj < pad + A_TQ)
                tab[o, 3, head_rows, :] = table(0)
                tab[o, 4, head_rows, :] = table(2 * pad)

    branch_refs = ((q1, k1, v1), (q4, k4, v4), (q16, k16, v16))
    n_sub = seq // SUB_DIL
    segments = []

    for o, ((_, dil), (q_ref, k_ref, v_ref)) in enumerate(zip(DILATED_CONFIGS, branch_refs)):
        n = seq // dil
        nbk = n // A_TQ
        assert n % A_TQ == 0 and nbk >= 2 and n >= A_BAND

        if dil == SUB_DIL:
            dst_acc, dst_max, dst_den = acc_n.at[o], max_n.at[o], den_n.at[o]
        else:
            dst_acc, dst_max, dst_den = acc_g, max_g, den_g

        def blocks(g):
            for u in range(A_GROUP):
                jb = g * A_GROUP + u
                row0 = jb * A_TQ
                band0 = jnp.clip(row0 - pad, 0, seq - A_BAND)
                yield u, jb, _aligned(row0, A_TQ), pl.multiple_of(band0, pad)

        def stage_qk(g, slot, o=o, nbk=nbk, q_ref=q_ref, k_ref=k_ref):
            for u, jb, row0, band0 in blocks(g):
                local = jb % nbk
                var = jnp.where(jb == 0, 3, jnp.where(jb == n_blocks - 1, 4, jnp.where(
                    local == 0, 1, jnp.where(local == nbk - 1, 2, 0))))
                s_bufs[slot, u] = _scores(q_ref[pl.ds(row0, A_TQ), :],
                                          k_ref[pl.ds(band0, A_BAND), :], tab[o, var], is_h0)

        def stage_softmax(g, slot, dst_max=dst_max):
            for u in range(A_GROUP):
                m_buf[u] = _row_max(s_bufs[slot, u])
            for u, _, row0, _ in blocks(g):
                m = m_buf[u]
                p_bufs[slot, u] = _probabilities(s_bufs[slot, u], m)
                dst_max[pl.ds(row0, A_TQ), :] = _unstack(m, is_h0)

        def stage_pv(g, slot, dst_acc=dst_acc, dst_den=dst_den, v_ref=v_ref):
            for u, _, row0, band0 in blocks(g):
                pv, den = _pv_and_sum(p_bufs[slot, u], v_ref[pl.ds(band0, A_BAND), :])
                dst_acc[pl.ds(row0, A_TQ), :] = _unstack(pv, is_h0)
                dst_den[pl.ds(row0, A_TQ), :] = _unstack(den, is_h0)

        if dil == 1:
            def to_merge_order(o=o):
                rows = min(n_sub, COPY_ROWS)

                def one_class(r, carry):
                    for part in range(n_sub // rows):
                        src = pl.ds(r + part * rows * SUB_DIL, rows, stride=SUB_DIL)
                        dst = pl.ds(pl.multiple_of(r * n_sub + part * rows, rows), rows)
                        acc_n[o, dst, :] = acc_g[src, :]
                        max_n[o, dst, :] = max_g[src, :]
                        den_n[o, dst, :] = den_g[src, :]
                    return carry

                lax.fori_loop(0, SUB_DIL, one_class, 0)
        elif dil == SUB_DIL ** 2:
            def to_merge_order(o=o, n=n, dil=dil):
                def one_class(c, carry):
                    src = pl.ds(pl.multiple_of(c * n, n), n)
                    dst = pl.ds((c // SUB_DIL) * n_sub + c % SUB_DIL, n, stride=SUB_DIL)
                    acc_n[o, dst, :] = acc_g[src, :]
                    max_n[o, dst, :] = max_g[src, :]
                    den_n[o, dst, :] = den_g[src, :]
                    return carry

                lax.fori_loop(0, dil, one_class, 0)
        else:
            to_merge_order = None

        segments.append((seq // (A_TQ * A_GROUP), stage_qk, stage_softmax, stage_pv,
                         to_merge_order))

    assert DILATED_CONFIGS[1][1] == SUB_DIL
    _three_stage_pipeline(segments)

    chunks_per_class = n_sub // COPY_ROWS

    def merge(c, carry):
        rows = pl.ds(pl.multiple_of(c * COPY_ROWS, COPY_ROWS), COPY_ROWS)
        m0, m1, m2 = max_n[0, rows, :], max_n[1, rows, :], max_n[2, rows, :]
        m = jnp.maximum(jnp.maximum(m0, m1), m2)
        w0, w1, w2 = jnp.exp2(m0 - m), jnp.exp2(m1 - m), jnp.exp2(m2 - m)
        num = w0 * acc_n[0, rows, :] + w1 * acc_n[1, rows, :] + w2 * acc_n[2, rows, :]
        den = w0 * den_n[0, rows, :] + w1 * den_n[1, rows, :] + w2 * den_n[2, rows, :]
        r, i0 = c // chunks_per_class, (c % chunks_per_class) * COPY_ROWS
        o_ref[pl.ds(r + SUB_DIL * i0, COPY_ROWS, stride=SUB_DIL), :] = num / den
        return carry

    lax.fori_loop(0, seq // COPY_ROWS, merge, 0)


def _attn_a(qkv, slopes):
    b, s, width = qkv[0].shape
    pairs = width // LANES
    blk = pl.BlockSpec((None, s, LANES), lambda hp, bi, *_: (bi, 0, hp))
    n_branch = len(DILATED_CONFIGS)
    assert s % (2 * A_TQ * A_GROUP) == 0 and s % (SUB_DIL * COPY_ROWS) == 0
    return pl.pallas_call(
        functools.partial(_attn_a_kernel, seq=s),
        grid_spec=pltpu.PrefetchScalarGridSpec(
            num_scalar_prefetch=1,
            grid=(pairs, b),
            in_specs=[blk] * len(qkv),
            out_specs=blk,
            scratch_shapes=[
                pltpu.VMEM((n_branch, 5, 2 * A_TQ, A_BAND), F32),
                pltpu.VMEM((2, A_GROUP, 2 * A_TQ, A_BAND), F32),
                pltpu.VMEM((2, A_GROUP, 2 * A_TQ, A_BAND), BF16),
                pltpu.VMEM((A_GROUP, 2 * A_TQ, LANES), F32),
                pltpu.VMEM((s, LANES), F32),
                pltpu.VMEM((s, LANES), F32),
                pltpu.VMEM((s, LANES), F32),
                pltpu.VMEM((n_branch, s, LANES), F32),
                pltpu.VMEM((n_branch, s, LANES), F32),
                pltpu.VMEM((n_branch, s, LANES), F32),
            ],
        ),
        out_shape=jax.ShapeDtypeStruct((b, s, width), F32),
        compiler_params=_params(("arbitrary", "arbitrary")),
        name="attn_dilated",
    )(slopes, *qkv)


def _na_variant_rows(n_rows):
    return (NA_ROWS // 2, 0, C_QROWS, n_rows - 2 * C_QROWS, n_rows - C_QROWS)


def _na_bias_tables(rpb, n_rows):
    n_heads, n_dr, n_dc = rpb.shape
    w = GRID_W
    left = w - 1 - (NA_COLS - 1)
    padded = jnp.pad(rpb.astype(F32) * LOG2E, ((0, 0), (0, 0), (left, 2 * w - left - n_dc)),
                     constant_values=NEG)
    skew = jnp.broadcast_to(padded[:, :, None, :], (n_heads, n_dr, w, 2 * w))
    skew = skew.reshape(n_heads, n_dr, 2 * w * w)[:, :, :w * (2 * w - 1)]
    toep = skew.reshape(n_heads, n_dr, w, 2 * w - 1)[:, :, :, w - 1:]
    qc = np.arange(w).reshape(-1, 1)
    kc = np.arange(w).reshape(1, -1)
    cstart = np.clip(qc - NA_COLS // 2, 0, w - NA_COLS)
    col_in = (kc >= cstart) & (kc < cstart + NA_COLS)
    toep = jnp.where(col_in, toep, NEG)
    masked = jnp.full((n_heads, w, w), NEG, F32)
    tables = []
    for r in _na_variant_rows(n_rows):
        base = int(np.clip(r - NA_ROWS // 2, 0, n_rows - C_KROWS))
        q_rows = []
        for a in range(C_QROWS):
            row = r + a
            rstart = int(np.clip(row - NA_ROWS // 2, 0, n_rows - NA_ROWS))
            tiles = []
            for i in range(C_KROWS):
                krow = base + i
                valid = rstart <= krow < rstart + NA_ROWS
                tiles.append(toep[:, krow - row + NA_ROWS - 1] if valid else masked)
            q_rows.append(jnp.concatenate(tiles, axis=-1))
        table = jnp.concatenate(q_rows, axis=-2)
        tables.append(table.reshape(n_heads // 2, 2 * C_QROWS * w, C_KROWS * w))
    return jnp.stack(tables, axis=1)


def _attn_c_kernel(q_ref, k_ref, v_ref, bias_ref, o_ref, s_bufs, p_bufs, m_buf, *, n_rows):
    lane = lax.broadcasted_iota(jnp.int32, (1, LANES), 1)
    is_h0 = lane < HEAD_DIM
    n_blocks = n_rows // C_QROWS
    tq = C_QROWS * GRID_W
    tk = C_KROWS * GRID_W
    assert n_blocks >= 5 and n_blocks % (2 * C_GROUP) == 0

    def blocks(g):
        for u in range(C_GROUP):
            p = g * C_GROUP + u
            base = jnp.clip(p * C_QROWS - NA_ROWS // 2, 0, n_rows - C_KROWS)
            k0 = base * GRID_W
            yield u, p, _aligned(p * tq, tq), pl.multiple_of(k0, GRID_W)

    def stage_qk(g, slot):
        for u, p, q0, k0 in blocks(g):
            var = jnp.where(p == 0, 1, jnp.where(p == 1, 2, jnp.where(
                p == n_blocks - 2, 3, jnp.where(p == n_blocks - 1, 4, 0))))
            s_bufs[slot, u] = _scores(q_ref[pl.ds(q0, tq), :], k_ref[pl.ds(k0, tk), :],
                                      bias_ref[var], is_h0)

    def stage_softmax(g, slot):
        for u in range(C_GROUP):
            m_buf[u] = _row_max(s_bufs[slot, u])
        for u in range(C_GROUP):
            p_bufs[slot, u] = _probabilities(s_bufs[slot, u], m_buf[u])

    def stage_pv(g, slot):
        for u, _, q0, k0 in blocks(g):
            pv, den = _pv_and_sum(p_bufs[slot, u], v_ref[pl.ds(k0, tk), :])
            o_ref[pl.ds(q0, tq), :] = _unstack(pv, is_h0) / _unstack(den, is_h0)

    _three_stage_pipeline([(n_blocks // C_GROUP, stage_qk, stage_softmax, stage_pv, None)])


def _attn_c(qc, kc, vc, bias):
    b, s, width = qc.shape
    pairs = width // LANES
    tq = C_QROWS * GRID_W
    tk = C_KROWS * GRID_W
    blk = pl.BlockSpec((None, s, LANES), lambda bi, hp: (bi, 0, hp))
    return pl.pallas_call(
        functools.partial(_attn_c_kernel, n_rows=s // GRID_W),
        grid=(b, pairs),
        in_specs=[blk, blk, blk,
                  pl.BlockSpec((None,) + bias.shape[1:], lambda bi, hp: (hp, 0, 0, 0))],
        out_specs=blk,
        out_shape=jax.ShapeDtypeStruct((b, s, width), F32),
        scratch_shapes=[
            pltpu.VMEM((2, C_GROUP, 2 * tq, tk), F32),
            pltpu.VMEM((2, C_GROUP, 2 * tq, tk), BF16),
            pltpu.VMEM((C_GROUP, 2 * tq, LANES), F32),
        ],
        compiler_params=_params(("arbitrary", "arbitrary")),
        name="attn_neighbourhood",
    )(qc, kc, vc, bias)


def _mix_ffn_kernel(oa_ref, oc_ref, u_ref, up_ref, un_ref, x_ref,
                    na_ref, nc_ref, wpool_ref, ps_ref, wout_ref, g1_ref,
                    nf_ref, sc2_ref, sh2_ref, w1_ref, w2_ref, g2_ref, nfin_ref,
                    o_ref, cat, act, ext, *, tm, seq, d_ff, final_norm):
    i = pl.program_id(1)
    halo = POOL_HALO

    width_b = u_ref.shape[-1]
    ext[0:halo, :] = jnp.where(i > 0, up_ref[...], 0.0)
    ext[halo:halo + tm, :] = u_ref[...]
    ext[halo + tm:halo + tm + halo, :] = jnp.where(i < pl.num_programs(1) - 1, un_ref[...], 0.0)
    lane = lax.broadcasted_iota(jnp.int32, (1, width_b), 1)
    wa = oa_ref.shape[-1]

    part = tm // FFN_PARTS
    starts = list(range(0, tm, part))
    mixed = {}

    def mixer(r0):
        rows = pl.ds(r0, part)
        t = i * tm + r0 + lax.broadcasted_iota(jnp.int32, (part, 1), 0)
        num = jnp.zeros((part, width_b), F32)
        den = jnp.zeros((part, width_b), F32)
        acc = jnp.zeros((part, width_b), F32)
        done = 0
        for g, w in enumerate(POOL_WINDOWS):
            for d in list(range(-(w // 2), -done)) + list(range(done, w // 2)):
                acc = acc + ext[halo + r0 + d:halo + r0 + d + part, :]
            done = w // 2
            lo = jnp.clip(t - w // 2, 0, seq - 1)
            hi = jnp.clip(t + w // 2 - 1, 0, seq - 1)
            cnt = (hi - lo + 1).astype(F32)
            in_group = (lane >= g * POOL_GROUP_DIM) & (lane < (g + 1) * POOL_GROUP_DIM)
            num = jnp.where(in_group, acc, num)
            den = jnp.where(in_group, cnt, den)
            yield
        pooled = num / den - u_ref[rows, :]
        y = jnp.dot(pooled.astype(BF16), wpool_ref[...], preferred_element_type=F32) * ps_ref[...]
        cat[rows, wa:wa + width_b] = y.astype(BF16)
        yield
        cat[rows, 0:wa] = _rms(oa_ref[rows, :], na_ref[...]).astype(BF16)
        yield
        cat[rows, wa + width_b:] = _rms(oc_ref[rows, :], nc_ref[...]).astype(BF16)
        yield
        mix = jnp.dot(cat[rows, :], wout_ref[...], preferred_element_type=F32)
        x1 = x_ref[rows, :] + g1_ref[...] * mix
        yield
        h2 = (_rms(x1, nf_ref[...]) * (1.0 + sc2_ref[...]) + sh2_ref[...]).astype(BF16)
        mixed[r0] = (x1, h2)

    def swiglu(r0, background):
        rows = pl.ds(r0, part)
        x1, h2 = mixed.pop(r0)
        for c in range(d_ff // FFN_CHUNK):
            lo = c * FFN_CHUNK
            gate = jnp.dot(h2, w1_ref[:, lo:lo + FFN_CHUNK], preferred_element_type=F32)
            up = jnp.dot(h2, w1_ref[:, d_ff + lo:d_ff + lo + FFN_CHUNK],
                         preferred_element_type=F32)
            act[rows, lo:lo + FFN_CHUNK] = (gate * jax.nn.sigmoid(gate) * up).astype(BF16)
            next(background, None)
        for _ in background:
            pass
        ffn = jnp.dot(act[rows, :], w2_ref[...], preferred_element_type=F32)
        x2 = x1 + g2_ref[...] * ffn
        if final_norm:
            x2 = _rms(x2, nfin_ref[...])
        o_ref[rows, :] = x2

    pieces = {r0: mixer(r0) for r0 in starts}
    for _ in pieces[starts[0]]:
        pass
    for k, r0 in enumerate(starts):
        swiglu(r0, pieces[starts[k + 1]] if k + 1 < len(starts) else iter(()))


def _mix_ffn(oa, oc, ub, x, norm_a, norm_c, w_pool_bd, pool_scale, w_out, mod,
             norm_ffn, w1, w2, norm_final, layer, final_norm):
    b, s, d = x.shape
    tm = FFN_TM
    d_ff = w2.shape[1]
    width_a, width_c, width_b = oa.shape[-1], oc.shape[-1], ub.shape[-1]
    assert d_ff % FFN_CHUNK == 0 and tm % POOL_HALO == 0
    hb = tm // POOL_HALO
    n_hb = s // POOL_HALO
    tile = lambda width: pl.BlockSpec((None, tm, width), lambda bi, i: (bi, i, 0))
    return pl.pallas_call(
        functools.partial(_mix_ffn_kernel, tm=tm, seq=s, d_ff=d_ff, final_norm=final_norm),
        grid=(b, s // tm),
        in_specs=[
            tile(width_a), tile(width_c), tile(width_b),
            pl.BlockSpec((None, POOL_HALO, width_b),
                         lambda bi, i: (bi, jnp.maximum(i * hb - 1, 0), 0)),
            pl.BlockSpec((None, POOL_HALO, width_b),
                         lambda bi, i: (bi, jnp.minimum((i + 1) * hb, n_hb - 1), 0)),
            tile(d),
            _layer_row(width_a, layer), _layer_row(width_c, layer),
            _layer_resident(w_pool_bd.shape, layer), _layer_row(width_b, layer),
            _layer_resident(w_out.shape, layer), _mod_spec(d, layer, 2),
            _layer_row(d, layer), _mod_spec(d, layer, 4), _mod_spec(d, layer, 3),
            _layer_resident(w1.shape, layer), _layer_resident(w2.shape, layer),
            _mod_spec(d, layer, 5),
            pl.BlockSpec((1, d), lambda bi, i: (0, 0)),
        ],
        out_specs=tile(d),
        out_shape=jax.ShapeDtypeStruct((b, s, d), F32),
        scratch_shapes=[
            pltpu.VMEM((tm, d), BF16),
            pltpu.VMEM((tm, d_ff), BF16),
            pltpu.VMEM((tm + 2 * POOL_HALO, width_b), F32),
        ],
        compiler_params=_params(("arbitrary", "arbitrary")),
        name="mix_ffn",
    )(oa, oc, ub, ub, ub, x, norm_a, norm_c, w_pool_bd, pool_scale, w_out, mod,
      norm_ffn, mod, mod, w1, w2, mod, norm_final.reshape(1, d))


def _block_diag(w_pool):
    depth, g, gd, _ = w_pool.shape
    out = jnp.zeros((depth, g * gd, g * gd), w_pool.dtype)
    for j in range(g):
        out = out.at[:, j * gd:(j + 1) * gd, j * gd:(j + 1) * gd].set(w_pool[:, j])
    return out


def kernel(x, c, w_ada, b_ada, norm_mix, w_in, norm_a_out, norm_c_out, w_pool, pool_scale, rpb,
           w_out, norm_ffn, w_ffn_in, w_ffn_out, norm_final):
    depth = w_ada.shape[0]
    b, s, d = x.shape
    assert s % GRID_W == 0 and s % PROJ_TM == 0 and s % FFN_TM == 0

    mod = _ada(c, w_ada, b_ada).reshape(depth, b, N_MOD, 1, d)
    heads = jnp.arange(N_HEADS_A, dtype=F32)
    slopes = 2.0 ** (-8.0 * (heads + 1.0) / N_HEADS_A)

    per_channel = lambda p: p.reshape(depth, 1, p.shape[-1])
    w_in_b, w_out_b = w_in.astype(BF16), w_out.astype(BF16)
    w1_b, w2_b = w_ffn_in.astype(BF16), w_ffn_out.astype(BF16)
    w_pool_b = _block_diag(w_pool).astype(BF16)
    norm_mix_r, norm_a_r, norm_c_r = per_channel(norm_mix), per_channel(norm_a_out), per_channel(norm_c_out)
    pool_scale_r, norm_ffn_r = per_channel(pool_scale), per_channel(norm_ffn)

    for l in range(depth):
        proj = _proj(x, norm_mix_r, mod, w_in_b, l)
        ub, qc, kc, vc = proj[N_REGROUPED:len(_PROJ_OUTS)]
        qkv = proj[:N_REGROUPED] + proj[len(_PROJ_OUTS):]
        oa = _attn_a([t.reshape(b, s, WIDTH_A) for t in qkv], slopes)
        oc = _attn_c(qc, kc, vc, _na_bias_tables(rpb[l], s // GRID_W))
        x = _mix_ffn(oa, oc, ub, x, norm_a_r, norm_c_r, w_pool_b, pool_scale_r, w_out_b, mod,
                     norm_ffn_r, w1_b, w2_b, norm_final, l, final_norm=(l == depth - 1))
    return x
```

```python
import functools

import numpy as np
import jax
import jax.numpy as jnp
from jax import lax
from jax.experimental import pallas as pl
from jax.experimental.pallas import tpu as pltpu

F32 = jnp.float32
BF16 = jnp.bfloat16

HEAD_DIM = 64
N_HEADS_A = 6
N_HEADS_C = 6
WIDTH_A = N_HEADS_A * HEAD_DIM
WIDTH_C = N_HEADS_C * HEAD_DIM
POOL_WINDOWS = (2, 4, 8, 16)
POOL_GROUP_DIM = 64
WIDTH_B = len(POOL_WINDOWS) * POOL_GROUP_DIM
DILATED_CONFIGS = ((128, 1), (512, 4), (2048, 16))
HALF_WINDOW = 64
GRID_W = 64
NA_ROWS = 8
NA_COLS = 16
N_MOD = 6
EPS = 1e-6
NEG = -1e30
LOG2E = 1.4426950408889634
Q_SCALE = HEAD_DIM ** -0.5 * LOG2E

LANES = 128
VMEM_LIMIT_BYTES = 60 * 1024 * 1024

PROJ_TM = 1024
PROJ_CHUNK = 512
FFN_TM = 512
FFN_CHUNK = 256
FFN_PARTS = 2
A_TQ = 128
A_BAND = A_TQ + 2 * HALF_WINDOW
A_GROUP = 8
C_QROWS = 2
C_KROWS = 10
C_GROUP = 2
POOL_HALO = 8
COPY_ROWS = 512


def _params(semantics):
    return pltpu.CompilerParams(dimension_semantics=semantics,
                                vmem_limit_bytes=VMEM_LIMIT_BYTES)


def _layer_resident(shape, layer):
    zeros = (0,) * (len(shape) - 1)
    return pl.BlockSpec((None,) + tuple(shape[1:]), lambda *_: (layer,) + zeros,
                        pipeline_mode=pl.Buffered(1))


def _layer_row(width, layer):
    return pl.BlockSpec((None, 1, width), lambda *_: (layer, 0, 0))


def _mod_spec(d, layer, j):
    return pl.BlockSpec((None, None, None, 1, d), lambda bi, i: (layer, bi, j, 0, 0))


def _rms(x, g):
    ms = jnp.mean(x * x, axis=-1, keepdims=True)
    return (x * lax.rsqrt(ms + EPS)) * g


def _ada_kernel(c_ref, w_ref, b_ref, o_ref):
    c = c_ref[...]
    act = c * jax.nn.sigmoid(c)
    o_ref[...] = jnp.dot(act, w_ref[...], preferred_element_type=F32) + b_ref[...]


def _ada(c, w_ada, b_ada):
    depth, d, six_d = w_ada.shape
    b = c.shape[0]
    n_chunks = six_d // d
    return pl.pallas_call(
        _ada_kernel,
        grid=(depth, n_chunks),
        in_specs=[
            pl.BlockSpec((b, d), lambda l, j: (0, 0)),
            pl.BlockSpec((None, d, d), lambda l, j: (l, 0, j)),
            pl.BlockSpec((None, 1, d), lambda l, j: (l, 0, j)),
        ],
        out_specs=pl.BlockSpec((None, b, d), lambda l, j: (l, 0, j)),
        out_shape=jax.ShapeDtypeStruct((depth, b, six_d), F32),
        compiler_params=_params(("arbitrary", "arbitrary")),
        name="ada",
    )(c, w_ada, b_ada.reshape(depth, 1, six_d))


_PROJ_OUTS = (
    ("qa", WIDTH_A, Q_SCALE, BF16),
    ("ka", WIDTH_A, None, BF16),
    ("va", WIDTH_A, None, BF16),
    ("ub", WIDTH_B, None, F32),
    ("qc", WIDTH_C, Q_SCALE, BF16),
    ("kc", WIDTH_C, None, BF16),
    ("vc", WIDTH_C, None, BF16),
)
N_REGROUPED = 3
SUB_DIL = 4
assert tuple(dil for _, dil in DILATED_CONFIGS) == (1, SUB_DIL, SUB_DIL ** 2)


def _proj_kernel(x_ref, g_ref, sc_ref, sh_ref, w_ref, *refs):
    n_out = len(_PROJ_OUTS)
    out_refs = refs[:n_out]
    o4_refs = refs[n_out:n_out + N_REGROUPED]
    o16_refs = refs[n_out + N_REGROUPED:n_out + 2 * N_REGROUPED]
    zs, z4 = refs[n_out + 2 * N_REGROUPED:]
    tm = x_ref.shape[0]

    h = _rms(x_ref[...], g_ref[...]) * (1.0 + sc_ref[...]) + sh_ref[...]
    h = h.astype(BF16)
    total = w_ref.shape[-1]
    n4, n16 = tm // SUB_DIL, tm // SUB_DIL ** 2
    tiles_per_out = WIDTH_A // LANES

    def regroup(j):
        o4, o16 = o4_refs[j // tiles_per_out], o16_refs[j // tiles_per_out]
        cols = pl.ds((j % tiles_per_out) * LANES, LANES)
        for r in range(SUB_DIL):
            t4 = zs[j, pl.ds(r, n4, stride=SUB_DIL), :]
            z4[j, r * n4:(r + 1) * n4, :] = t4
            o4[r, :, cols] = t4.astype(BF16)
        for r in range(SUB_DIL):
            for r2 in range(SUB_DIL):
                t16 = z4[j, pl.ds(r * n4 + r2, n16, stride=SUB_DIL), :]
                o16[SUB_DIL * r + r2, :, cols] = t16.astype(BF16)

    for c0 in range(0, total, PROJ_CHUNK):
        c1 = min(c0 + PROJ_CHUNK, total)
        z = jnp.dot(h, w_ref[:, c0:c1], preferred_element_type=F32)
        lo = 0
        for idx, ((_, width, scale, dtype), o_ref) in enumerate(zip(_PROJ_OUTS, out_refs)):
            a, b = max(lo, c0), min(lo + width, c1)
            if a < b:
                piece = z[:, a - c0:b - c0]
                if scale is not None:
                    piece = piece * scale
                o_ref[:, a - lo:b - lo] = piece.astype(dtype)
                if idx < N_REGROUPED:
                    for col in range(a, b, LANES):
                        zs[col // LANES] = piece[:, col - a:col - a + LANES]
            lo += width
    for j in range(N_REGROUPED * tiles_per_out):
        regroup(j)


def _proj(x, norm, mod, w_in, layer):
    b, s, d = x.shape
    tm = PROJ_TM
    assert tm % (16 * SUB_DIL ** 2) == 0
    nat = [pl.BlockSpec((None, tm, width), lambda bi, i: (bi, i, 0)) for _, width, _, _ in _PROJ_OUTS]
    nat_shapes = [jax.ShapeDtypeStruct((b, s, width), dtype) for _, width, _, dtype in _PROJ_OUTS]
    regrouped, regrouped_shapes = [], []
    for classes in (SUB_DIL, SUB_DIL ** 2):
        for _ in range(N_REGROUPED):
            regrouped.append(pl.BlockSpec((None, classes, tm // classes, WIDTH_A),
                                          lambda bi, i: (bi, 0, i, 0)))
            regrouped_shapes.append(jax.ShapeDtypeStruct((b, classes, s // classes, WIDTH_A), BF16))
    n_tiles = N_REGROUPED * WIDTH_A // LANES
    return pl.pallas_call(
        _proj_kernel,
        grid=(b, s // tm),
        in_specs=[
            pl.BlockSpec((None, tm, d), lambda bi, i: (bi, i, 0)),
            _layer_row(d, layer),
            _mod_spec(d, layer, 1), _mod_spec(d, layer, 0),
            _layer_resident(w_in.shape, layer),
        ],
        out_specs=nat + regrouped,
        out_shape=nat_shapes + regrouped_shapes,
        scratch_shapes=[
            pltpu.VMEM((n_tiles, tm, LANES), F32),
            pltpu.VMEM((n_tiles, tm, LANES), F32),
        ],
        compiler_params=_params(("arbitrary", "arbitrary")),
        name="proj",
    )(x, norm, mod, mod, w_in)


def _stack_heads(qb, is_h0):
    zero = jnp.zeros_like(qb)
    return jnp.concatenate([jnp.where(is_h0, qb, zero), jnp.where(is_h0, zero, qb)], axis=0)


def _unstack(x, is_h0):
    rows = x.shape[0] // 2
    return jnp.where(is_h0, x[:rows], x[rows:])


def _scores(qb, kb, bias, is_h0):
    return lax.dot_general(_stack_heads(qb, is_h0), kb, (((1,), (1,)), ((), ())),
                           preferred_element_type=F32) + bias


def _row_max(s):
    return jnp.broadcast_to(jnp.max(s, axis=-1, keepdims=True), (s.shape[0], LANES))


def _probabilities(s, m):
    return jnp.exp2(s - jnp.tile(m, (1, s.shape[1] // LANES))).astype(BF16)


def _pv_and_sum(p, vb):
    ones = jnp.ones_like(vb)
    both = jnp.dot(p, jnp.concatenate([vb, ones], axis=1), preferred_element_type=F32)
    return both[:, :LANES], both[:, LANES:]


def _three_stage_pipeline(segments):
    finish = None
    for n_groups, stage_qk, stage_softmax, stage_pv, after in segments:
        assert n_groups % 2 == 0 and n_groups >= 2
        stage_qk(0, 0)
        if finish is not None:
            finish[0]()
        stage_qk(1, 1)
        stage_softmax(0, 0)
        if finish is not None:
            finish[1]()

        def two_steps(t, carry, stage_qk=stage_qk, stage_softmax=stage_softmax, stage_pv=stage_pv):
            g = 2 * t + 1
            stage_qk(g + 1, 0)
            stage_softmax(g, 1)
            stage_pv(g - 1, 0)
            stage_qk(g + 2, 1)
            stage_softmax(g + 1, 0)
            stage_pv(g, 1)
            return carry

        lax.fori_loop(0, (n_groups - 2) // 2, two_steps, 0)
        last = n_groups - 1

        def first_trailing(stage_softmax=stage_softmax, stage_pv=stage_pv, last=last):
            stage_softmax(last, 1)
            stage_pv(last - 1, 0)

        def second_trailing(stage_pv=stage_pv, after=after, last=last):
            stage_pv(last, 1)
            if after is not None:
                after()

        finish = (first_trailing, second_trailing)
    finish[0]()
    finish[1]()


def _aligned(start, align):
    return start if isinstance(start, int) else pl.multiple_of(start, align)


def _attn_a_kernel(slopes_ref, q1, k1, v1, q4, k4, v4, q16, k16, v16, o_ref,
                   tab, s_bufs, p_bufs, m_buf, acc_g, max_g, den_g,
                   acc_n, max_n, den_n, *, seq):
    hp = pl.program_id(0)
    lane = lax.broadcasted_iota(jnp.int32, (1, LANES), 1)
    is_h0 = lane < HEAD_DIM
    pad = HALF_WINDOW
    n_blocks = seq // A_TQ

    @pl.when(pl.program_id(1) == 0)
    def _build_tables():
        qi = lax.broadcasted_iota(jnp.int32, (A_TQ, A_BAND), 0)
        kj = lax.broadcasted_iota(jnp.int32, (A_TQ, A_BAND), 1)
        for o, (_, dil) in enumerate(DILATED_CONFIGS):
            for hh in range(2):
                slope = slopes_ref[2 * hp + hh]
                head_rows = pl.ds(hh * A_TQ, A_TQ)

                def table(band_lead, keep=None, slope=slope, dil=dil):
                    arel = jnp.abs(kj - band_lead - qi)
                    ok = arel <= HALF_WINDOW
                    if keep is not None:
                        ok = ok & keep
                    return jnp.where(ok, -(slope * (arel * dil).astype(F32)) * LOG2E, NEG)

                tab[o, 0, head_rows, :] = table(pad)
                tab[o, 1, head_rows, :] = table(pad, kj >= pad)
                tab[o, 2, head_rows, :] = table(pad, kj < pad + A_TQ)
                tab[o, 3, head_rows, :] = table(0)
                tab[o, 4, head_rows, :] = table(2 * pad)

    branch_refs = ((q1, k1, v1), (q4, k4, v4), (q16, k16, v16))
    n_sub = seq // SUB_DIL
    segments = []

    for o, ((_, dil), (q_ref, k_ref, v_ref)) in enumerate(zip(DILATED_CONFIGS, branch_refs)):
        n = seq // dil
        nbk = n // A_TQ
        assert n % A_TQ == 0 and nbk >= 2 and n >= A_BAND

        if dil == SUB_DIL:
            dst_acc, dst_max, dst_den = acc_n.at[o], max_n.at[o], den_n.at[o]
        else:
            dst_acc, dst_max, dst_den = acc_g, max_g, den_g

        def blocks(g):
            for u in range(A_GROUP):
                jb = g * A_GROUP + u
                row0 = jb * A_TQ
                band0 = jnp.clip(row0 - pad, 0, seq - A_BAND)
                yield u, jb, _aligned(row0, A_TQ), pl.multiple_of(band0, pad)

        def stage_qk(g, slot, o=o, nbk=nbk, q_ref=q_ref, k_ref=k_ref):
            for u, jb, row0, band0 in blocks(g):
                local = jb % nbk
                var = jnp.where(jb == 0, 3, jnp.where(jb == n_blocks - 1, 4, jnp.where(
                    local == 0, 1, jnp.where(local == nbk - 1, 2, 0))))
                s_bufs[slot, u] = _scores(q_ref[pl.ds(row0, A_TQ), :],
                                          k_ref[pl.ds(band0, A_BAND), :], tab[o, var], is_h0)

        def stage_softmax(g, slot, dst_max=dst_max):
            for u in range(A_GROUP):
                m_buf[u] = _row_max(s_bufs[slot, u])
            for u, _, row0, _ in blocks(g):
                m = m_buf[u]
                p_bufs[slot, u] = _probabilities(s_bufs[slot, u], m)
                dst_max[pl.ds(row0, A_TQ), :] = _unstack(m, is_h0)

        def stage_pv(g, slot, dst_acc=dst_acc, dst_den=dst_den, v_ref=v_ref):
            for u, _, row0, band0 in blocks(g):
                pv, den = _pv_and_sum(p_bufs[slot, u], v_ref[pl.ds(band0, A_BAND), :])
                dst_acc[pl.ds(row0, A_TQ), :] = _unstack(pv, is_h0)
                dst_den[pl.ds(row0, A_TQ), :] = _unstack(den, is_h0)

        if dil == 1:
            def to_merge_order(o=o):
                rows = min(n_sub, COPY_ROWS)

                def one_class(r, carry):
                    for part in range(n_sub // rows):
                        src = pl.ds(r + part * rows * SUB_DIL, rows, stride=SUB_DIL)
                        dst = pl.ds(pl.multiple_of(r * n_sub + part * rows, rows), rows)
                        acc_n[o, dst, :] = acc_g[src, :]
                        max_n[o, dst, :] = max_g[src, :]
                        den_n[o, dst, :] = den_g[src, :]
                    return carry

                lax.fori_loop(0, SUB_DIL, one_class, 0)
        elif dil == SUB_DIL ** 2:
            def to_merge_order(o=o, n=n, dil=dil):
                def one_class(c, carry):
                    src = pl.ds(pl.multiple_of(c * n, n), n)
                    dst = pl.ds((c // SUB_DIL) * n_sub + c % SUB_DIL, n, stride=SUB_DIL)
                    acc_n[o, dst, :] = acc_g[src, :]
                    max_n[o, dst, :] = max_g[src, :]
                    den_n[o, dst, :] = den_g[src, :]
                    return carry

                lax.fori_loop(0, dil, one_class, 0)
        else:
            to_merge_order = None

        segments.append((seq // (A_TQ * A_GROUP), stage_qk, stage_softmax, stage_pv,
                         to_merge_order))

    assert DILATED_CONFIGS[1][1] == SUB_DIL
    _three_stage_pipeline(segments)

    chunks_per_class = n_sub // COPY_ROWS

    def merge(c, carry):
        rows = pl.ds(pl.multiple_of(c * COPY_ROWS, COPY_ROWS), COPY_ROWS)
        m0, m1, m2 = max_n[0, rows, :], max_n[1, rows, :], max_n[2, rows, :]
        m = jnp.maximum(jnp.maximum(m0, m1), m2)
        w0, w1, w2 = jnp.exp2(m0 - m), jnp.exp2(m1 - m), jnp.exp2(m2 - m)
        num = w0 * acc_n[0, rows, :] + w1 * acc_n[1, rows, :] + w2 * acc_n[2, rows, :]
        den = w0 * den_n[0, rows, :] + w1 * den_n[1, rows, :] + w2 * den_n[2, rows, :]
        r, i0 = c // chunks_per_class, (c % chunks_per_class) * COPY_ROWS
        o_ref[pl.ds(r + SUB_DIL * i0, COPY_ROWS, stride=SUB_DIL), :] = num / den
        return carry

    lax.fori_loop(0, seq // COPY_ROWS, merge, 0)


def _attn_a(qkv, slopes):
    b, s, width = qkv[0].shape
    pairs = width // LANES
    blk = pl.BlockSpec((None, s, LANES), lambda hp, bi, *_: (bi, 0, hp))
    n_branch = len(DILATED_CONFIGS)
    assert s % (2 * A_TQ * A_GROUP) == 0 and s % (SUB_DIL * COPY_ROWS) == 0
    return pl.pallas_call(
        functools.partial(_attn_a_kernel, seq=s),
        grid_spec=pltpu.PrefetchScalarGridSpec(
            num_scalar_prefetch=1,
            grid=(pairs, b),
            in_specs=[blk] * len(qkv),
            out_specs=blk,
            scratch_shapes=[
                pltpu.VMEM((n_branch, 5, 2 * A_TQ, A_BAND), F32),
                pltpu.VMEM((2, A_GROUP, 2 * A_TQ, A_BAND), F32),
                pltpu.VMEM((2, A_GROUP, 2 * A_TQ, A_BAND), BF16),
                pltpu.VMEM((A_GROUP, 2 * A_TQ, LANES), F32),
                pltpu.VMEM((s, LANES), F32),
                pltpu.VMEM((s, LANES), F32),
                pltpu.VMEM((s, LANES), F32),
                pltpu.VMEM((n_branch, s, LANES), F32),
                pltpu.VMEM((n_branch, s, LANES), F32),
                pltpu.VMEM((n_branch, s, LANES), F32),
            ],
        ),
        out_shape=jax.ShapeDtypeStruct((b, s, width), F32),
        compiler_params=_params(("arbitrary", "arbitrary")),
        name="attn_dilated",
    )(slopes, *qkv)


def _na_variant_rows(n_rows):
    return (NA_ROWS // 2, 0, C_QROWS, n_rows - 2 * C_QROWS, n_rows - C_QROWS)


def _na_bias_tables(rpb, n_rows):
    n_heads, n_dr, n_dc = rpb.shape
    w = GRID_W
    left = w - 1 - (NA_COLS - 1)
    padded = jnp.pad(rpb.astype(F32) * LOG2E, ((0, 0), (0, 0), (left, 2 * w - left - n_dc)),
                     constant_values=NEG)
    skew = jnp.broadcast_to(padded[:, :, None, :], (n_heads, n_dr, w, 2 * w))
    skew = skew.reshape(n_heads, n_dr, 2 * w * w)[:, :, :w * (2 * w - 1)]
    toep = skew.reshape(n_heads, n_dr, w, 2 * w - 1)[:, :, :, w - 1:]
    qc = np.arange(w).reshape(-1, 1)
    kc = np.arange(w).reshape(1, -1)
    cstart = np.clip(qc - NA_COLS // 2, 0, w - NA_COLS)
    col_in = (kc >= cstart) & (kc < cstart + NA_COLS)
    toep = jnp.where(col_in, toep, NEG)
    masked = jnp.full((n_heads, w, w), NEG, F32)
    tables = []
    for r in _na_variant_rows(n_rows):
        base = int(np.clip(r - NA_ROWS // 2, 0, n_rows - C_KROWS))
        q_rows = []
        for a in range(C_QROWS):
            row = r + a
            rstart = int(np.clip(row - NA_ROWS // 2, 0, n_rows - NA_ROWS))
            tiles = []
            for i in range(C_KROWS):
                krow = base + i
                valid = rstart <= krow < rstart + NA_ROWS
                tiles.append(toep[:, krow - row + NA_ROWS - 1] if valid else masked)
            q_rows.append(jnp.concatenate(tiles, axis=-1))
        table = jnp.concatenate(q_rows, axis=-2)
        tables.append(table.reshape(n_heads // 2, 2 * C_QROWS * w, C_KROWS * w))
    return jnp.stack(tables, axis=1)


def _attn_c_kernel(q_ref, k_ref, v_ref, bias_ref, o_ref, s_bufs, p_bufs, m_buf, *, n_rows):
    lane = lax.broadcasted_iota(jnp.int32, (1, LANES), 1)
    is_h0 = lane < HEAD_DIM
    n_blocks = n_rows // C_QROWS
    tq = C_QROWS * GRID_W
    tk = C_KROWS * GRID_W
    assert n_blocks >= 5 and n_blocks % (2 * C_GROUP) == 0

    def blocks(g):
        for u in range(C_GROUP):
            p = g * C_GROUP + u
            base = jnp.clip(p * C_QROWS - NA_ROWS // 2, 0, n_rows - C_KROWS)
            k0 = base * GRID_W
            yield u, p, _aligned(p * tq, tq), pl.multiple_of(k0, GRID_W)

    def stage_qk(g, slot):
        for u, p, q0, k0 in blocks(g):
            var = jnp.where(p == 0, 1, jnp.where(p == 1, 2, jnp.where(
                p == n_blocks - 2, 3, jnp.where(p == n_blocks - 1, 4, 0))))
            s_bufs[slot, u] = _scores(q_ref[pl.ds(q0, tq), :], k_ref[pl.ds(k0, tk), :],
                                      bias_ref[var], is_h0)

    def stage_softmax(g, slot):
        for u in range(C_GROUP):
            m_buf[u] = _row_max(s_bufs[slot, u])
        for u in range(C_GROUP):
            p_bufs[slot, u] = _probabilities(s_bufs[slot, u], m_buf[u])

    def stage_pv(g, slot):
        for u, _, q0, k0 in blocks(g):
            pv, den = _pv_and_sum(p_bufs[slot, u], v_ref[pl.ds(k0, tk), :])
            o_ref[pl.ds(q0, tq), :] = _unstack(pv, is_h0) / _unstack(den, is_h0)

    _three_stage_pipeline([(n_blocks // C_GROUP, stage_qk, stage_softmax, stage_pv, None)])


def _attn_c(qc, kc, vc, bias):
    b, s, width = qc.shape
    pairs = width // LANES
    tq = C_QROWS * GRID_W
    tk = C_KROWS * GRID_W
    blk = pl.BlockSpec((None, s, LANES), lambda bi, hp: (bi, 0, hp))
    return pl.pallas_call(
        functools.partial(_attn_c_kernel, n_rows=s // GRID_W),
        grid=(b, pairs),
        in_specs=[blk, blk, blk,
                  pl.BlockSpec((None,) + bias.shape[1:], lambda bi, hp: (hp, 0, 0, 0))],
        out_specs=blk,
        out_shape=jax.ShapeDtypeStruct((b, s, width), F32),
        scratch_shapes=[
            pltpu.VMEM((2, C_GROUP, 2 * tq, tk), F32),
            pltpu.VMEM((2, C_GROUP, 2 * tq, tk), BF16),
            pltpu.VMEM((C_GROUP, 2 * tq, LANES), F32),
        ],
        compiler_params=_params(("arbitrary", "arbitrary")),
        name="attn_neighbourhood",
    )(qc, kc, vc, bias)


def _mix_ffn_kernel(oa_ref, oc_ref, u_ref, up_ref, un_ref, x_ref,
                    na_ref, nc_ref, wpool_ref, ps_ref, wout_ref, g1_ref,
                    nf_ref, sc2_ref, sh2_ref, w1_ref, w2_ref, g2_ref, nfin_ref,
                    o_ref, cat, act, ext, *, tm, seq, d_ff, final_norm):
    i = pl.program_id(1)
    halo = POOL_HALO

    width_b = u_ref.shape[-1]
    ext[0:halo, :] = jnp.where(i > 0, up_ref[...], 0.0)
    ext[halo:halo + tm, :] = u_ref[...]
    ext[halo + tm:halo + tm + halo, :] = jnp.where(i < pl.num_programs(1) - 1, un_ref[...], 0.0)
    lane = lax.broadcasted_iota(jnp.int32, (1, width_b), 1)
    wa = oa_ref.shape[-1]

    part = tm // FFN_PARTS
    starts = list(range(0, tm, part))
    mixed = {}

    def mixer(r0):
        rows = pl.ds(r0, part)
        t = i * tm + r0 + lax.broadcasted_iota(jnp.int32, (part, 1), 0)
        num = jnp.zeros((part, width_b), F32)
        den = jnp.zeros((part, width_b), F32)
        acc = jnp.zeros((part, width_b), F32)
        done = 0
        for g, w in enumerate(POOL_WINDOWS):
            for d in list(range(-(w // 2), -done)) + list(range(done, w // 2)):
                acc = acc + ext[halo + r0 + d:halo + r0 + d + part, :]
            done = w // 2
            lo = jnp.clip(t - w // 2, 0, seq - 1)
            hi = jnp.clip(t + w // 2 - 1, 0, seq - 1)
            cnt = (hi - lo + 1).astype(F32)
            in_group = (lane >= g * POOL_GROUP_DIM) & (lane < (g + 1) * POOL_GROUP_DIM)
            num = jnp.where(in_group, acc, num)
            den = jnp.where(in_group, cnt, den)
            yield
        pooled = num / den - u_ref[rows, :]
        y = jnp.dot(pooled.astype(BF16), wpool_ref[...], preferred_element_type=F32) * ps_ref[...]
        cat[rows, wa:wa + width_b] = y.astype(BF16)
        yield
        cat[rows, 0:wa] = _rms(oa_ref[rows, :], na_ref[...]).astype(BF16)
        yield
        cat[rows, wa + width_b:] = _rms(oc_ref[rows, :], nc_ref[...]).astype(BF16)
        yield
        mix = jnp.dot(cat[rows, :], wout_ref[...], preferred_element_type=F32)
        x1 = x_ref[rows, :] + g1_ref[...] * mix
        yield
        h2 = (_rms(x1, nf_ref[...]) * (1.0 + sc2_ref[...]) + sh2_ref[...]).astype(BF16)
        mixed[r0] = (x1, h2)

    def swiglu(r0, background):
        rows = pl.ds(r0, part)
        x1, h2 = mixed.pop(r0)
        for c in range(d_ff // FFN_CHUNK):
            lo = c * FFN_CHUNK
            gate = jnp.dot(h2, w1_ref[:, lo:lo + FFN_CHUNK], preferred_element_type=F32)
            up = jnp.dot(h2, w1_ref[:, d_ff + lo:d_ff + lo + FFN_CHUNK],
                         preferred_element_type=F32)
            act[rows, lo:lo + FFN_CHUNK] = (gate * jax.nn.sigmoid(gate) * up).astype(BF16)
            next(background, None)
        for _ in background:
            pass
        ffn = jnp.dot(act[rows, :], w2_ref[...], preferred_element_type=F32)
        x2 = x1 + g2_ref[...] * ffn
        if final_norm:
            x2 = _rms(x2, nfin_ref[...])
        o_ref[rows, :] = x2

    pieces = {r0: mixer(r0) for r0 in starts}
    for _ in pieces[starts[0]]:
        pass
    for k, r0 in enumerate(starts):
        swiglu(r0, pieces[starts[k + 1]] if k + 1 < len(starts) else iter(()))


def _mix_ffn(oa, oc, ub, x, norm_a, norm_c, w_pool_bd, pool_scale, w_out, mod,
             norm_ffn, w1, w2, norm_final, layer, final_norm):
    b, s, d = x.shape
    tm = FFN_TM
    d_ff = w2.shape[1]
    width_a, width_c, width_b = oa.shape[-1], oc.shape[-1], ub.shape[-1]
    assert d_ff % FFN_CHUNK == 0 and tm % POOL_HALO == 0
    hb = tm // POOL_HALO
    n_hb = s // POOL_HALO
    tile = lambda width: pl.BlockSpec((None, tm, width), lambda bi, i: (bi, i, 0))
    return pl.pallas_call(
        functools.partial(_mix_ffn_kernel, tm=tm, seq=s, d_ff=d_ff, final_norm=final_norm),
        grid=(b, s // tm),
        in_specs=[
            tile(width_a), tile(width_c), tile(width_b),
            pl.BlockSpec((None, POOL_HALO, width_b),
                         lambda bi, i: (bi, jnp.maximum(i * hb - 1, 0), 0)),
            pl.BlockSpec((None, POOL_HALO, width_b),
                         lambda bi, i: (bi, jnp.minimum((i + 1) * hb, n_hb - 1), 0)),
            tile(d),
            _layer_row(width_a, layer), _layer_row(width_c, layer),
            _layer_resident(w_pool_bd.shape, layer), _layer_row(width_b, layer),
            _layer_resident(w_out.shape, layer), _mod_spec(d, layer, 2),
            _layer_row(d, layer), _mod_spec(d, layer, 4), _mod_spec(d, layer, 3),
            _layer_resident(w1.shape, layer), _layer_resident(w2.shape, layer),
            _mod_spec(d, layer, 5),
            pl.BlockSpec((1, d), lambda bi, i: (0, 0)),
        ],
        out_specs=tile(d),
        out_shape=jax.ShapeDtypeStruct((b, s, d), F32),
        scratch_shapes=[
            pltpu.VMEM((tm, d), BF16),
            pltpu.VMEM((tm, d_ff), BF16),
            pltpu.VMEM((tm + 2 * POOL_HALO, width_b), F32),
        ],
        compiler_params=_params(("arbitrary", "arbitrary")),
        name="mix_ffn",
    )(oa, oc, ub, ub, ub, x, norm_a, norm_c, w_pool_bd, pool_scale, w_out, mod,
      norm_ffn, mod, mod, w1, w2, mod, norm_final.reshape(1, d))


def _block_diag(w_pool):
    depth, g, gd, _ = w_pool.shape
    out = jnp.zeros((depth, g * gd, g * gd), w_pool.dtype)
    for j in range(g):
        out = out.at[:, j * gd:(j + 1) * gd, j * gd:(j + 1) * gd].set(w_pool[:, j])
    return out


def kernel(x, c, w_ada, b_ada, norm_mix, w_in, norm_a_out, norm_c_out, w_pool, pool_scale, rpb,
           w_out, norm_ffn, w_ffn_in, w_ffn_out, norm_final):
    depth = w_ada.shape[0]
    b, s, d = x.shape
    assert s % GRID_W == 0 and s % PROJ_TM == 0 and s % FFN_TM == 0

    mod = _ada(c, w_ada, b_ada).reshape(depth, b, N_MOD, 1, d)
    heads = jnp.arange(N_HEADS_A, dtype=F32)
    slopes = 2.0 ** (-8.0 * (heads + 1.0) / N_HEADS_A)

    per_channel = lambda p: p.reshape(depth, 1, p.shape[-1])
    w_in_b, w_out_b = w_in.astype(BF16), w_out.astype(BF16)
    w1_b, w2_b = w_ffn_in.astype(BF16), w_ffn_out.astype(BF16)
    w_pool_b = _block_diag(w_pool).astype(BF16)
    norm_mix_r, norm_a_r, norm_c_r = per_channel(norm_mix), per_channel(norm_a_out), per_channel(norm_c_out)
    pool_scale_r, norm_ffn_r = per_channel(pool_scale), per_channel(norm_ffn)

    for l in range(depth):
        proj = _proj(x, norm_mix_r, mod, w_in_b, l)
        ub, qc, kc, vc = proj[N_REGROUPED:len(_PROJ_OUTS)]
        qkv = proj[:N_REGROUPED] + proj[len(_PROJ_OUTS):]
        oa = _attn_a([t.reshape(b, s, WIDTH_A) for t in qkv], slopes)
        oc = _attn_c(qc, kc, vc, _na_bias_tables(rpb[l], s // GRID_W))
        x = _mix_ffn(oa, oc, ub, x, norm_a_r, norm_c_r, w_pool_b, pool_scale_r, w_out_b, mod,
                     norm_ffn_r, w1_b, w2_b, norm_final, l, final_norm=(l == depth - 1))
    return x
```

```python
import functools

import numpy as np
import jax
import jax.numpy as jnp
from jax import lax
from jax.experimental import pallas as pl
from jax.experimental.pallas import tpu as pltpu

F32 = jnp.float32
BF16 = jnp.bfloat16

HEAD_DIM = 64
N_HEADS_A = 6
N_HEADS_C = 6
WIDTH_A = N_HEADS_A * HEAD_DIM
WIDTH_C = N_HEADS_C * HEAD_DIM
POOL_WINDOWS = (2, 4, 8, 16)
POOL_GROUP_DIM = 64
WIDTH_B = len(POOL_WINDOWS) * POOL_GROUP_DIM
DILATED_CONFIGS = ((128, 1), (512, 4), (2048, 16))
HALF_WINDOW = 64
GRID_W = 64
NA_ROWS = 8
NA_COLS = 16
N_MOD = 6
EPS = 1e-6
NEG = -1e30
LOG2E = 1.4426950408889634
Q_SCALE = HEAD_DIM ** -0.5 * LOG2E

LANES = 128
VMEM_LIMIT_BYTES = 60 * 1024 * 1024

PROJ_TM = 512
PROJ_CHUNK = 512
FFN_TM = 512
FFN_CHUNK = 256
FFN_PARTS = 2
A_TQ = 128
A_BAND = A_TQ + 2 * HALF_WINDOW
A_GROUP = 8
C_QROWS = 2
C_KROWS = 10
C_GROUP = 2
POOL_HALO = 8
COPY_ROWS = 512


def _params(semantics):
    return pltpu.CompilerParams(dimension_semantics=semantics,
                                vmem_limit_bytes=VMEM_LIMIT_BYTES)


def _layer_resident(shape, layer):
    zeros = (0,) * (len(shape) - 1)
    return pl.BlockSpec((None,) + tuple(shape[1:]), lambda *_: (layer,) + zeros,
                        pipeline_mode=pl.Buffered(1))


def _layer_row(width, layer):
    return pl.BlockSpec((None, 1, width), lambda *_: (layer, 0, 0))


def _mod_spec(d, layer, j):
    return pl.BlockSpec((None, None, None, 1, d), lambda bi, i: (layer, bi, j, 0, 0))


def _rms(x, g):
    ms = jnp.mean(x * x, axis=-1, keepdims=True)
    return (x * lax.rsqrt(ms + EPS)) * g


def _ada_kernel(c_ref, w_ref, b_ref, o_ref):
    c = c_ref[...]
    act = c * jax.nn.sigmoid(c)
    o_ref[...] = jnp.dot(act, w_ref[...], preferred_element_type=F32) + b_ref[...]


def _ada(c, w_ada, b_ada):
    depth, d, six_d = w_ada.shape
    b = c.shape[0]
    n_chunks = six_d // d
    return pl.pallas_call(
        _ada_kernel,
        grid=(depth, n_chunks),
        in_specs=[
            pl.BlockSpec((b, d), lambda l, j: (0, 0)),
            pl.BlockSpec((None, d, d), lambda l, j: (l, 0, j)),
            pl.BlockSpec((None, 1, d), lambda l, j: (l, 0, j)),
        ],
        out_specs=pl.BlockSpec((None, b, d), lambda l, j: (l, 0, j)),
        out_shape=jax.ShapeDtypeStruct((depth, b, six_d), F32),
        compiler_params=_params(("arbitrary", "arbitrary")),
        name="ada",
    )(c, w_ada, b_ada.reshape(depth, 1, six_d))


_PROJ_OUTS = (
    ("qa", WIDTH_A, Q_SCALE, BF16),
    ("ka", WIDTH_A, None, BF16),
    ("va", WIDTH_A, None, BF16),
    ("ub", WIDTH_B, None, F32),
    ("qc", WIDTH_C, Q_SCALE, BF16),
    ("kc", WIDTH_C, None, BF16),
    ("vc", WIDTH_C, None, BF16),
)
N_REGROUPED = 3
SUB_DIL = 4
assert tuple(dil for _, dil in DILATED_CONFIGS) == (1, SUB_DIL, SUB_DIL ** 2)


def _proj_kernel(x_ref, g_ref, sc_ref, sh_ref, w_ref, *refs):
    n_out = len(_PROJ_OUTS)
    out_refs = refs[:n_out]
    o4_refs = refs[n_out:n_out + N_REGROUPED]
    o16_refs = refs[n_out + N_REGROUPED:n_out + 2 * N_REGROUPED]
    zs, z4 = refs[n_out + 2 * N_REGROUPED:]
    tm = x_ref.shape[0]

    h = _rms(x_ref[...], g_ref[...]) * (1.0 + sc_ref[...]) + sh_ref[...]
    h = h.astype(BF16)
    total = w_ref.shape[-1]
    n4, n16 = tm // SUB_DIL, tm // SUB_DIL ** 2
    tiles_per_out = WIDTH_A // LANES

    def regroup(j):
        o4, o16 = o4_refs[j // tiles_per_out], o16_refs[j // tiles_per_out]
        cols = pl.ds((j % tiles_per_out) * LANES, LANES)
        for r in range(SUB_DIL):
            t4 = zs[j, pl.ds(r, n4, stride=SUB_DIL), :]
            z4[j, r * n4:(r + 1) * n4, :] = t4
            o4[r, :, cols] = t4.astype(BF16)
        for r in range(SUB_DIL):
            for r2 in range(SUB_DIL):
                t16 = z4[j, pl.ds(r * n4 + r2, n16, stride=SUB_DIL), :]
                o16[SUB_DIL * r + r2, :, cols] = t16.astype(BF16)

    for c0 in range(0, total, PROJ_CHUNK):
        c1 = min(c0 + PROJ_CHUNK, total)
        z = jnp.dot(h, w_ref[:, c0:c1], preferred_element_type=F32)
        lo = 0
        for idx, ((_, width, scale, dtype), o_ref) in enumerate(zip(_PROJ_OUTS, out_refs)):
            a, b = max(lo, c0), min(lo + width, c1)
            if a < b:
                piece = z[:, a - c0:b - c0]
                if scale is not None:
                    piece = piece * scale
                o_ref[:, a - lo:b - lo] = piece.astype(dtype)
                if idx < N_REGROUPED:
                    for col in range(a, b, LANES):
                        zs[col // LANES] = piece[:, col - a:col - a + LANES]
            lo += width
    for j in range(N_REGROUPED * tiles_per_out):
        regroup(j)


def _proj(x, norm, mod, w_in, layer):
    b, s, d = x.shape
    tm = PROJ_TM
    assert tm % (16 * SUB_DIL ** 2) == 0
    nat = [pl.BlockSpec((None, tm, width), lambda bi, i: (bi, i, 0)) for _, width, _, _ in _PROJ_OUTS]
    nat_shapes = [jax.ShapeDtypeStruct((b, s, width), dtype) for _, width, _, dtype in _PROJ_OUTS]
    regrouped, regrouped_shapes = [], []
    for classes in (SUB_DIL, SUB_DIL ** 2):
        for _ in range(N_REGROUPED):
            regrouped.append(pl.BlockSpec((None, classes, tm // classes, WIDTH_A),
                                          lambda bi, i: (bi, 0, i, 0)))
            regrouped_shapes.append(jax.ShapeDtypeStruct((b, classes, s // classes, WIDTH_A), BF16))
    n_tiles = N_REGROUPED * WIDTH_A // LANES
    return pl.pallas_call(
        _proj_kernel,
        grid=(b, s // tm),
        in_specs=[
            pl.BlockSpec((None, tm, d), lambda bi, i: (bi, i, 0)),
            _layer_row(d, layer),
            _mod_spec(d, layer, 1), _mod_spec(d, layer, 0),
            _layer_resident(w_in.shape, layer),
        ],
        out_specs=nat + regrouped,
        out_shape=nat_shapes + regrouped_shapes,
        scratch_shapes=[
            pltpu.VMEM((n_tiles, tm, LANES), F32),
            pltpu.VMEM((n_tiles, tm, LANES), F32),
        ],
        compiler_params=_params(("arbitrary", "arbitrary")),
        name="proj",
    )(x, norm, mod, mod, w_in)


def _stack_heads(qb, is_h0):
    zero = jnp.zeros_like(qb)
    return jnp.concatenate([jnp.where(is_h0, qb, zero), jnp.where(is_h0, zero, qb)], axis=0)


def _unstack(x, is_h0):
    rows = x.shape[0] // 2
    return jnp.where(is_h0, x[:rows], x[rows:])


def _scores(qb, kb, bias, is_h0):
    return lax.dot_general(_stack_heads(qb, is_h0), kb, (((1,), (1,)), ((), ())),
                           preferred_element_type=F32) + bias


def _row_max(s):
    return jnp.broadcast_to(jnp.max(s, axis=-1, keepdims=True), (s.shape[0], LANES))


def _probabilities(s, m):
    return jnp.exp2(s - jnp.tile(m, (1, s.shape[1] // LANES))).astype(BF16)


def _pv_and_sum(p, vb):
    ones = jnp.ones_like(vb)
    both = jnp.dot(p, jnp.concatenate([vb, ones], axis=1), preferred_element_type=F32)
    return both[:, :LANES], both[:, LANES:]


def _three_stage_pipeline(segments):
    finish = None
    for n_groups, stage_qk, stage_softmax, stage_pv, after in segments:
        assert n_groups % 2 == 0 and n_groups >= 2
        stage_qk(0, 0)
        if finish is not None:
            finish[0]()
        stage_qk(1, 1)
        stage_softmax(0, 0)
        if finish is not None:
            finish[1]()

        def two_steps(t, carry, stage_qk=stage_qk, stage_softmax=stage_softmax, stage_pv=stage_pv):
            g = 2 * t + 1
            stage_qk(g + 1, 0)
            stage_softmax(g, 1)
            stage_pv(g - 1, 0)
            stage_qk(g + 2, 1)
            stage_softmax(g + 1, 0)
            stage_pv(g, 1)
            return carry

        lax.fori_loop(0, (n_groups - 2) // 2, two_steps, 0)
        last = n_groups - 1

        def first_trailing(stage_softmax=stage_softmax, stage_pv=stage_pv, last=last):
            stage_softmax(last, 1)
            stage_pv(last - 1, 0)

        def second_trailing(stage_pv=stage_pv, after=after, last=last):
            stage_pv(last, 1)
            if after is not None:
                after()

        finish = (first_trailing, second_trailing)
    finish[0]()
    finish[1]()


def _aligned(start, align):
    return start if isinstance(start, int) else pl.multiple_of(start, align)


def _attn_a_kernel(slopes_ref, q1, k1, v1, q4, k4, v4, q16, k16, v16, o_ref,
                   tab, s_bufs, p_bufs, m_buf, acc_g, max_g, den_g,
                   acc_n, max_n, den_n, *, seq):
    hp = pl.program_id(0)
    lane = lax.broadcasted_iota(jnp.int32, (1, LANES), 1)
    is_h0 = lane < HEAD_DIM
    pad = HALF_WINDOW
    n_blocks = seq // A_TQ

    @pl.when(pl.program_id(1) == 0)
    def _build_tables():
        qi = lax.broadcasted_iota(jnp.int32, (A_TQ, A_BAND), 0)
        kj = lax.broadcasted_iota(jnp.int32, (A_TQ, A_BAND), 1)
        for o, (_, dil) in enumerate(DILATED_CONFIGS):
            for hh in range(2):
                slope = slopes_ref[2 * hp + hh]
                head_rows = pl.ds(hh * A_TQ, A_TQ)

                def table(band_lead, keep=None, slope=slope, dil=dil):
                    arel = jnp.abs(kj - band_lead - qi)
                    ok = arel <= HALF_WINDOW
                    if keep is not None:
                        ok = ok & keep
                    return jnp.where(ok, -(slope * (arel * dil).astype(F32)) * LOG2E, NEG)

                tab[o, 0, head_rows, :] = table(pad)
                tab[o, 1, head_rows, :] = table(pad, kj >= pad)
                tab[o, 2, head_rows, :] = table(pad, kj < pad + A_TQ)
                tab[o, 3, head_rows, :] = table(0)
                tab[o, 4, head_rows, :] = table(2 * pad)

    branch_refs = ((q1, k1, v1), (q4, k4, v4), (q16, k16, v16))
    n_sub = seq // SUB_DIL
    segments = []

    for o, ((_, dil), (q_ref, k_ref, v_ref)) in enumerate(zip(DILATED_CONFIGS, branch_refs)):
        n = seq // dil
        nbk = n // A_TQ
        assert n % A_TQ == 0 and nbk >= 2 and n >= A_BAND

        if dil == SUB_DIL:
            dst_acc, dst_max, dst_den = acc_n.at[o], max_n.at[o], den_n.at[o]
        else:
            dst_acc, dst_max, dst_den = acc_g, max_g, den_g

        def blocks(g):
            for u in range(A_GROUP):
                jb = g * A_GROUP + u
                row0 = jb * A_TQ
                band0 = jnp.clip(row0 - pad, 0, seq - A_BAND)
                yield u, jb, _aligned(row0, A_TQ), pl.multiple_of(band0, pad)

        def stage_qk(g, slot, o=o, nbk=nbk, q_ref=q_ref, k_ref=k_ref):
            for u, jb, row0, band0 in blocks(g):
                local = jb % nbk
                var = jnp.where(jb == 0, 3, jnp.where(jb == n_blocks - 1, 4, jnp.where(
                    local == 0, 1, jnp.where(local == nbk - 1, 2, 0))))
                s_bufs[slot, u] = _scores(q_ref[pl.ds(row0, A_TQ), :],
                                          k_ref[pl.ds(band0, A_BAND), :], tab[o, var], is_h0)

        def stage_softmax(g, slot, dst_max=dst_max):
            for u in range(A_GROUP):
                m_buf[u] = _row_max(s_bufs[slot, u])
            for u, _, row0, _ in blocks(g):
                m = m_buf[u]
                p_bufs[slot, u] = _probabilities(s_bufs[slot, u], m)
                dst_max[pl.ds(row0, A_TQ), :] = _unstack(m, is_h0)

        def stage_pv(g, slot, dst_acc=dst_acc, dst_den=dst_den, v_ref=v_ref):
            for u, _, row0, band0 in blocks(g):
                pv, den = _pv_and_sum(p_bufs[slot, u], v_ref[pl.ds(band0, A_BAND), :])
                dst_acc[pl.ds(row0, A_TQ), :] = _unstack(pv, is_h0)
                dst_den[pl.ds(row0, A_TQ), :] = _unstack(den, is_h0)

        if dil == 1:
            def to_merge_order(o=o):
                rows = min(n_sub, COPY_ROWS)

                def one_class(r, carry):
                    for part in range(n_sub // rows):
                        src = pl.ds(r + part * rows * SUB_DIL, rows, stride=SUB_DIL)
                        dst = pl.ds(pl.multiple_of(r * n_sub + part * rows, rows), rows)
                        acc_n[o, dst, :] = acc_g[src, :]
                        max_n[o, dst, :] = max_g[src, :]
                        den_n[o, dst, :] = den_g[src, :]
                    return carry

                lax.fori_loop(0, SUB_DIL, one_class, 0)
        elif dil == SUB_DIL ** 2:
            def to_merge_order(o=o, n=n, dil=dil):
                def one_class(c, carry):
                    src = pl.ds(pl.multiple_of(c * n, n), n)
                    dst = pl.ds((c // SUB_DIL) * n_sub + c % SUB_DIL, n, stride=SUB_DIL)
                    acc_n[o, dst, :] = acc_g[src, :]
                    max_n[o, dst, :] = max_g[src, :]
                    den_n[o, dst, :] = den_g[src, :]
                    return carry

                lax.fori_loop(0, dil, one_class, 0)
        else:
            to_merge_order = None

        segments.append((seq // (A_TQ * A_GROUP), stage_qk, stage_softmax, stage_pv,
                         to_merge_order))

    assert DILATED_CONFIGS[1][1] == SUB_DIL
    _three_stage_pipeline(segments)

    chunks_per_class = n_sub // COPY_ROWS

    def merge(c, carry):
        rows = pl.ds(pl.multiple_of(c * COPY_ROWS, COPY_ROWS), COPY_ROWS)
        m0, m1, m2 = max_n[0, rows, :], max_n[1, rows, :], max_n[2, rows, :]
        m = jnp.maximum(jnp.maximum(m0, m1), m2)
        w0, w1, w2 = jnp.exp2(m0 - m), jnp.exp2(m1 - m), jnp.exp2(m2 - m)
        num = w0 * acc_n[0, rows, :] + w1 * acc_n[1, rows, :] + w2 * acc_n[2, rows, :]
        den = w0 * den_n[0, rows, :] + w1 * den_n[1, rows, :] + w2 * den_n[2, rows, :]
        r, i0 = c // chunks_per_class, (c % chunks_per_class) * COPY_ROWS
        o_ref[pl.ds(r + SUB_DIL * i0, COPY_ROWS, stride=SUB_DIL), :] = num / den
        return carry

    lax.fori_loop(0, seq // COPY_ROWS, merge, 0)


def _attn_a(qkv, slopes):
    b, s, width = qkv[0].shape
    pairs = width // LANES
    blk = pl.BlockSpec((None, s, LANES), lambda hp, bi, *_: (bi, 0, hp))
    n_branch = len(DILATED_CONFIGS)
    assert s % (2 * A_TQ * A_GROUP) == 0 and s % (SUB_DIL * COPY_ROWS) == 0
    return pl.pallas_call(
        functools.partial(_attn_a_kernel, seq=s),
        grid_spec=pltpu.PrefetchScalarGridSpec(
            num_scalar_prefetch=1,
            grid=(pairs, b),
            in_specs=[blk] * len(qkv),
            out_specs=blk,
            scratch_shapes=[
                pltpu.VMEM((n_branch, 5, 2 * A_TQ, A_BAND), F32),
                pltpu.VMEM((2, A_GROUP, 2 * A_TQ, A_BAND), F32),
                pltpu.VMEM((2, A_GROUP, 2 * A_TQ, A_BAND), BF16),
                pltpu.VMEM((A_GROUP, 2 * A_TQ, LANES), F32),
                pltpu.VMEM((s, LANES), F32),
                pltpu.VMEM((s, LANES), F32),
                pltpu.VMEM((s, LANES), F32),
                pltpu.VMEM((n_branch, s, LANES), F32),
                pltpu.VMEM((n_branch, s, LANES), F32),
                pltpu.VMEM((n_branch, s, LANES), F32),
            ],
        ),
        out_shape=jax.ShapeDtypeStruct((b, s, width), F32),
        compiler_params=_params(("arbitrary", "arbitrary")),
        name="attn_dilated",
    )(slopes, *qkv)


def _na_variant_rows(n_rows):
    return (NA_ROWS // 2, 0, C_QROWS, n_rows - 2 * C_QROWS, n_rows - C_QROWS)


def _na_bias_tables(rpb, n_rows):
    n_heads, n_dr, n_dc = rpb.shape
    w = GRID_W
    left = w - 1 - (NA_COLS - 1)
    padded = jnp.pad(rpb.astype(F32) * LOG2E, ((0, 0), (0, 0), (left, 2 * w - left - n_dc)),
                     constant_values=NEG)
    skew = jnp.broadcast_to(padded[:, :, None, :], (n_heads, n_dr, w, 2 * w))
    skew = skew.reshape(n_heads, n_dr, 2 * w * w)[:, :, :w * (2 * w - 1)]
    toep = skew.reshape(n_heads, n_dr, w, 2 * w - 1)[:, :, :, w - 1:]
    qc = np.arange(w).reshape(-1, 1)
    kc = np.arange(w).reshape(1, -1)
    cstart = np.clip(qc - NA_COLS // 2, 0, w - NA_COLS)
    col_in = (kc >= cstart) & (kc < cstart + NA_COLS)
    toep = jnp.where(col_in, toep, NEG)
    masked = jnp.full((n_heads, w, w), NEG, F32)
    tables = []
    for r in _na_variant_rows(n_rows):
        base = int(np.clip(r - NA_ROWS // 2, 0, n_rows - C_KROWS))
        q_rows = []
        for a in range(C_QROWS):
            row = r + a
            rstart = int(np.clip(row - NA_ROWS // 2, 0, n_rows - NA_ROWS))
            tiles = []
            for i in range(C_KROWS):
                krow = base + i
                valid = rstart <= krow < rstart + NA_ROWS
                tiles.append(toep[:, krow - row + NA_ROWS - 1] if valid else masked)
            q_rows.append(jnp.concatenate(tiles, axis=-1))
        table = jnp.concatenate(q_rows, axis=-2)
        tables.append(table.reshape(n_heads // 2, 2 * C_QROWS * w, C_KROWS * w))
    return jnp.stack(tables, axis=1)


def _attn_c_kernel(q_ref, k_ref, v_ref, bias_ref, o_ref, s_bufs, p_bufs, m_buf, *, n_rows):
    lane = lax.broadcasted_iota(jnp.int32, (1, LANES), 1)
    is_h0 = lane < HEAD_DIM
    n_blocks = n_rows // C_QROWS
    tq = C_QROWS * GRID_W
    tk = C_KROWS * GRID_W
    assert n_blocks >= 5 and n_blocks % (2 * C_GROUP) == 0

    def blocks(g):
        for u in range(C_GROUP):
            p = g * C_GROUP + u
            base = jnp.clip(p * C_QROWS - NA_ROWS // 2, 0, n_rows - C_KROWS)
            k0 = base * GRID_W
            yield u, p, _aligned(p * tq, tq), pl.multiple_of(k0, GRID_W)

    def stage_qk(g, slot):
        for u, p, q0, k0 in blocks(g):
            var = jnp.where(p == 0, 1, jnp.where(p == 1, 2, jnp.where(
                p == n_blocks - 2, 3, jnp.where(p == n_blocks - 1, 4, 0))))
            s_bufs[slot, u] = _scores(q_ref[pl.ds(q0, tq), :], k_ref[pl.ds(k0, tk), :],
                                      bias_ref[var], is_h0)

    def stage_softmax(g, slot):
        for u in range(C_GROUP):
            m_buf[u] = _row_max(s_bufs[slot, u])
        for u in range(C_GROUP):
            p_bufs[slot, u] = _probabilities(s_bufs[slot, u], m_buf[u])

    def stage_pv(g, slot):
        for u, _, q0, k0 in blocks(g):
            pv, den = _pv_and_sum(p_bufs[slot, u], v_ref[pl.ds(k0, tk), :])
            o_ref[pl.ds(q0, tq), :] = _unstack(pv, is_h0) / _unstack(den, is_h0)

    _three_stage_pipeline([(n_blocks // C_GROUP, stage_qk, stage_softmax, stage_pv, None)])


def _attn_c(qc, kc, vc, bias):
    b, s, width = qc.shape
    pairs = width // LANES
    tq = C_QROWS * GRID_W
    tk = C_KROWS * GRID_W
    blk = pl.BlockSpec((None, s, LANES), lambda bi, hp: (bi, 0, hp))
    return pl.pallas_call(
        functools.partial(_attn_c_kernel, n_rows=s // GRID_W),
        grid=(b, pairs),
        in_specs=[blk, blk, blk,
                  pl.BlockSpec((None,) + bias.shape[1:], lambda bi, hp: (hp, 0, 0, 0))],
        out_specs=blk,
        out_shape=jax.ShapeDtypeStruct((b, s, width), F32),
        scratch_shapes=[
            pltpu.VMEM((2, C_GROUP, 2 * tq, tk), F32),
            pltpu.VMEM((2, C_GROUP, 2 * tq, tk), BF16),
            pltpu.VMEM((C_GROUP, 2 * tq, LANES), F32),
        ],
        compiler_params=_params(("arbitrary", "arbitrary")),
        name="attn_neighbourhood",
    )(qc, kc, vc, bias)


def _mix_ffn_kernel(oa_ref, oc_ref, u_ref, up_ref, un_ref, x_ref,
                    na_ref, nc_ref, wpool_ref, ps_ref, wout_ref, g1_ref,
                    nf_ref, sc2_ref, sh2_ref, w1_ref, w2_ref, g2_ref, nfin_ref,
                    o_ref, cat, act, ext, *, tm, seq, d_ff, final_norm):
    i = pl.program_id(1)
    halo = POOL_HALO

    width_b = u_ref.shape[-1]
    ext[0:halo, :] = jnp.where(i > 0, up_ref[...], 0.0)
    ext[halo:halo + tm, :] = u_ref[...]
    ext[halo + tm:halo + tm + halo, :] = jnp.where(i < pl.num_programs(1) - 1, un_ref[...], 0.0)
    lane = lax.broadcasted_iota(jnp.int32, (1, width_b), 1)
    wa = oa_ref.shape[-1]

    part = tm // FFN_PARTS
    starts = list(range(0, tm, part))
    mixed = {}

    def mixer(r0):
        rows = pl.ds(r0, part)
        t = i * tm + r0 + lax.broadcasted_iota(jnp.int32, (part, 1), 0)
        num = jnp.zeros((part, width_b), F32)
        den = jnp.zeros((part, width_b), F32)
        acc = jnp.zeros((part, width_b), F32)
        done = 0
        for g, w in enumerate(POOL_WINDOWS):
            for d in list(range(-(w // 2), -done)) + list(range(done, w // 2)):
                acc = acc + ext[halo + r0 + d:halo + r0 + d + part, :]
            done = w // 2
            lo = jnp.clip(t - w // 2, 0, seq - 1)
            hi = jnp.clip(t + w // 2 - 1, 0, seq - 1)
            cnt = (hi - lo + 1).astype(F32)
            in_group = (lane >= g * POOL_GROUP_DIM) & (lane < (g + 1) * POOL_GROUP_DIM)
            num = jnp.where(in_group, acc, num)
            den = jnp.where(in_group, cnt, den)
            yield
        pooled = num / den - u_ref[rows, :]
        y = jnp.dot(pooled.astype(BF16), wpool_ref[...], preferred_element_type=F32) * ps_ref[...]
        cat[rows, wa:wa + width_b] = y.astype(BF16)
        yield
        cat[rows, 0:wa] = _rms(oa_ref[rows, :], na_ref[...]).astype(BF16)
        yield
        cat[rows, wa + width_b:] = _rms(oc_ref[rows, :], nc_ref[...]).astype(BF16)
        yield
        mix = jnp.dot(cat[rows, :], wout_ref[...], preferred_element_type=F32)
        x1 = x_ref[rows, :] + g1_ref[...] * mix
        yield
        h2 = (_rms(x1, nf_ref[...]) * (1.0 + sc2_ref[...]) + sh2_ref[...]).astype(BF16)
        mixed[r0] = (x1, h2)

    def swiglu(r0, background):
        rows = pl.ds(r0, part)
        x1, h2 = mixed.pop(r0)
        for c in range(d_ff // FFN_CHUNK):
            lo = c * FFN_CHUNK
            gate = jnp.dot(h2, w1_ref[:, lo:lo + FFN_CHUNK], preferred_element_type=F32)
            up = jnp.dot(h2, w1_ref[:, d_ff + lo:d_ff + lo + FFN_CHUNK],
                         preferred_element_type=F32)
            act[rows, lo:lo + FFN_CHUNK] = (gate * jax.nn.sigmoid(gate) * up).astype(BF16)
            next(background, None)
        for _ in background:
            pass
        ffn = jnp.dot(act[rows, :], w2_ref[...], preferred_element_type=F32)
        x2 = x1 + g2_ref[...] * ffn
        if final_norm:
            x2 = _rms(x2, nfin_ref[...])
        o_ref[rows, :] = x2

    pieces = {r0: mixer(r0) for r0 in starts}
    for _ in pieces[starts[0]]:
        pass
    for k, r0 in enumerate(starts):
        swiglu(r0, pieces[starts[k + 1]] if k + 1 < len(starts) else iter(()))


def _mix_ffn(oa, oc, ub, x, norm_a, norm_c, w_pool_bd, pool_scale, w_out, mod,
             norm_ffn, w1, w2, norm_final, layer, final_norm):
    b, s, d = x.shape
    tm = FFN_TM
    d_ff = w2.shape[1]
    width_a, width_c, width_b = oa.shape[-1], oc.shape[-1], ub.shape[-1]
    assert d_ff % FFN_CHUNK == 0 and tm % POOL_HALO == 0
    hb = tm // POOL_HALO
    n_hb = s // POOL_HALO
    tile = lambda width: pl.BlockSpec((None, tm, width), lambda bi, i: (bi, i, 0))
    return pl.pallas_call(
        functools.partial(_mix_ffn_kernel, tm=tm, seq=s, d_ff=d_ff, final_norm=final_norm),
        grid=(b, s // tm),
        in_specs=[
            tile(width_a), tile(width_c), tile(width_b),
            pl.BlockSpec((None, POOL_HALO, width_b),
                         lambda bi, i: (bi, jnp.maximum(i * hb - 1, 0), 0)),
            pl.BlockSpec((None, POOL_HALO, width_b),
                         lambda bi, i: (bi, jnp.minimum((i + 1) * hb, n_hb - 1), 0)),
            tile(d),
            _layer_row(width_a, layer), _layer_row(width_c, layer),
            _layer_resident(w_pool_bd.shape, layer), _layer_row(width_b, layer),
            _layer_resident(w_out.shape, layer), _mod_spec(d, layer, 2),
            _layer_row(d, layer), _mod_spec(d, layer, 4), _mod_spec(d, layer, 3),
            _layer_resident(w1.shape, layer), _layer_resident(w2.shape, layer),
            _mod_spec(d, layer, 5),
            pl.BlockSpec((1, d), lambda bi, i: (0, 0)),
        ],
        out_specs=tile(d),
        out_shape=jax.ShapeDtypeStruct((b, s, d), F32),
        scratch_shapes=[
            pltpu.VMEM((tm, d), BF16),
            pltpu.VMEM((tm, d_ff), BF16),
            pltpu.VMEM((tm + 2 * POOL_HALO, width_b), F32),
        ],
        compiler_params=_params(("arbitrary", "arbitrary")),
        name="mix_ffn",
    )(oa, oc, ub, ub, ub, x, norm_a, norm_c, w_pool_bd, pool_scale, w_out, mod,
      norm_ffn, mod, mod, w1, w2, mod, norm_final.reshape(1, d))


def _block_diag(w_pool):
    depth, g, gd, _ = w_pool.shape
    out = jnp.zeros((depth, g * gd, g * gd), w_pool.dtype)
    for j in range(g):
        out = out.at[:, j * gd:(j + 1) * gd, j * gd:(j + 1) * gd].set(w_pool[:, j])
    return out


def kernel(x, c, w_ada, b_ada, norm_mix, w_in, norm_a_out, norm_c_out, w_pool, pool_scale, rpb,
           w_out, norm_ffn, w_ffn_in, w_ffn_out, norm_final):
    depth = w_ada.shape[0]
    b, s, d = x.shape
    assert s % GRID_W == 0 and s % PROJ_TM == 0 and s % FFN_TM == 0

    mod = _ada(c, w_ada, b_ada).reshape(depth, b, N_MOD, 1, d)
    heads = jnp.arange(N_HEADS_A, dtype=F32)
    slopes = 2.0 ** (-8.0 * (heads + 1.0) / N_HEADS_A)

    per_channel = lambda p: p.reshape(depth, 1, p.shape[-1])
    w_in_b, w_out_b = w_in.astype(BF16), w_out.astype(BF16)
    w1_b, w2_b = w_ffn_in.astype(BF16), w_ffn_out.astype(BF16)
    w_pool_b = _block_diag(w_pool).astype(BF16)
    norm_mix_r, norm_a_r, norm_c_r = per_channel(norm_mix), per_channel(norm_a_out), per_channel(norm_c_out)
    pool_scale_r, norm_ffn_r = per_channel(pool_scale), per_channel(norm_ffn)

    for l in range(depth):
        proj = _proj(x, norm_mix_r, mod, w_in_b, l)
        ub, qc, kc, vc = proj[N_REGROUPED:len(_PROJ_OUTS)]
        qkv = proj[:N_REGROUPED] + proj[len(_PROJ_OUTS):]
        oa = _attn_a([t.reshape(b, s, WIDTH_A) for t in qkv], slopes)
        oc = _attn_c(qc, kc, vc, _na_bias_tables(rpb[l], s // GRID_W))
        x = _mix_ffn(oa, oc, ub, x, norm_a_r, norm_c_r, w_pool_b, pool_scale_r, w_out_b, mod,
                     norm_ffn_r, w1_b, w2_b, norm_final, l, final_norm=(l == depth - 1))
    return x
```

```python
import functools

import numpy as np
import jax
import jax.numpy as jnp
from jax import lax
from jax.experimental import pallas as pl
from jax.experimental.pallas import tpu as pltpu

F32 = jnp.float32
BF16 = jnp.bfloat16

HEAD_DIM = 64
N_HEADS_A = 6
N_HEADS_C = 6
WIDTH_A = N_HEADS_A * HEAD_DIM
WIDTH_C = N_HEADS_C * HEAD_DIM
POOL_WINDOWS = (2, 4, 8, 16)
POOL_GROUP_DIM = 64
WIDTH_B = len(POOL_WINDOWS) * POOL_GROUP_DIM
DILATED_CONFIGS = ((128, 1), (512, 4), (2048, 16))
HALF_WINDOW = 64
GRID_W = 64
NA_ROWS = 8
NA_COLS = 16
N_MOD = 6
EPS = 1e-6
NEG = -1e30
LOG2E = 1.4426950408889634
Q_SCALE = HEAD_DIM ** -0.5 * LOG2E

LANES = 128
VMEM_LIMIT_BYTES = 60 * 1024 * 1024

PROJ_TM = 512
PROJ_CHUNK = 512
FFN_TM = 512
FFN_CHUNK = 256
FFN_PARTS = 2
A_TQ = 128
A_BAND = A_TQ + 2 * HALF_WINDOW
A_GROUP = 8
C_QROWS = 2
C_KROWS = 10
C_GROUP = 2
POOL_HALO = 8
COPY_ROWS = 512


def _params(semantics):
    return pltpu.CompilerParams(dimension_semantics=semantics,
                                vmem_limit_bytes=VMEM_LIMIT_BYTES)


def _layer_resident(shape, layer):
    zeros = (0,) * (len(shape) - 1)
    return pl.BlockSpec((None,) + tuple(shape[1:]), lambda *_: (layer,) + zeros,
                        pipeline_mode=pl.Buffered(1))


def _layer_row(width, layer):
    return pl.BlockSpec((None, 1, width), lambda *_: (layer, 0, 0))


def _mod_spec(d, layer, j):
    return pl.BlockSpec((None, None, None, 1, d), lambda bi, i: (layer, bi, j, 0, 0))


def _rms(x, g):
    ms = jnp.mean(x * x, axis=-1, keepdims=True)
    return (x * lax.rsqrt(ms + EPS)) * g


def _ada_kernel(c_ref, w_ref, b_ref, o_ref):
    c = c_ref[...]
    act = c * jax.nn.sigmoid(c)
    o_ref[...] = jnp.dot(act, w_ref[...], preferred_element_type=F32) + b_ref[...]


def _ada(c, w_ada, b_ada):
    depth, d, six_d = w_ada.shape
    b = c.shape[0]
    n_chunks = six_d // d
    return pl.pallas_call(
        _ada_kernel,
        grid=(depth, n_chunks),
        in_specs=[
            pl.BlockSpec((b, d), lambda l, j: (0, 0)),
            pl.BlockSpec((None, d, d), lambda l, j: (l, 0, j)),
            pl.BlockSpec((None, 1, d), lambda l, j: (l, 0, j)),
        ],
        out_specs=pl.BlockSpec((None, b, d), lambda l, j: (l, 0, j)),
        out_shape=jax.ShapeDtypeStruct((depth, b, six_d), F32),
        compiler_params=_params(("arbitrary", "arbitrary")),
        name="ada",
    )(c, w_ada, b_ada.reshape(depth, 1, six_d))


_PROJ_OUTS = (
    ("qa", WIDTH_A, Q_SCALE, BF16),
    ("ka", WIDTH_A, None, BF16),
    ("va", WIDTH_A, None, BF16),
    ("ub", WIDTH_B, None, F32),
    ("qc", WIDTH_C, Q_SCALE, BF16),
    ("kc", WIDTH_C, None, BF16),
    ("vc", WIDTH_C, None, BF16),
)
N_REGROUPED = 3
SUB_DIL = 4
assert tuple(dil for _, dil in DILATED_CONFIGS) == (1, SUB_DIL, SUB_DIL ** 2)


def _proj_kernel(x_ref, g_ref, sc_ref, sh_ref, w_ref, *refs):
    n_out = len(_PROJ_OUTS)
    out_refs = refs[:n_out]
    o4_refs = refs[n_out:n_out + N_REGROUPED]
    o16_refs = refs[n_out + N_REGROUPED:n_out + 2 * N_REGROUPED]
    zs, z4 = refs[n_out + 2 * N_REGROUPED:]
    tm = x_ref.shape[0]

    h = _rms(x_ref[...], g_ref[...]) * (1.0 + sc_ref[...]) + sh_ref[...]
    h = h.astype(BF16)
    total = w_ref.shape[-1]
    n4, n16 = tm // SUB_DIL, tm // SUB_DIL ** 2
    tiles_per_out = WIDTH_A // LANES

    def regroup(j):
        o4, o16 = o4_refs[j // tiles_per_out], o16_refs[j // tiles_per_out]
        cols = pl.ds((j % tiles_per_out) * LANES, LANES)
        for r in range(SUB_DIL):
            t4 = zs[j, pl.ds(r, n4, stride=SUB_DIL), :]
            z4[j, r * n4:(r + 1) * n4, :] = t4
            o4[r, :, cols] = t4.astype(BF16)
        for r in range(SUB_DIL):
            for r2 in range(SUB_DIL):
                t16 = z4[j, pl.ds(r * n4 + r2, n16, stride=SUB_DIL), :]
                o16[SUB_DIL * r + r2, :, cols] = t16.astype(BF16)

    for c0 in range(0, total, PROJ_CHUNK):
        c1 = min(c0 + PROJ_CHUNK, total)
        z = jnp.dot(h, w_ref[:, c0:c1], preferred_element_type=F32)
        lo = 0
        for idx, ((_, width, scale, dtype), o_ref) in enumerate(zip(_PROJ_OUTS, out_refs)):
            a, b = max(lo, c0), min(lo + width, c1)
            if a < b:
                piece = z[:, a - c0:b - c0]
                if scale is not None:
                    piece = piece * scale
                o_ref[:, a - lo:b - lo] = piece.astype(dtype)
                if idx < N_REGROUPED:
                    for col in range(a, b, LANES):
                        zs[col // LANES] = piece[:, col - a:col - a + LANES]
            lo += width
    for j in range(N_REGROUPED * tiles_per_out):
        regroup(j)


def _proj(x, norm, mod, w_in, layer):
    b, s, d = x.shape
    tm = PROJ_TM
    assert tm % (16 * SUB_DIL ** 2) == 0
    nat = [pl.BlockSpec((None, tm, width), lambda bi, i: (bi, i, 0)) for _, width, _, _ in _PROJ_OUTS]
    nat_shapes = [jax.ShapeDtypeStruct((b, s, width), dtype) for _, width, _, dtype in _PROJ_OUTS]
    regrouped, regrouped_shapes = [], []
    for classes in (SUB_DIL, SUB_DIL ** 2):
        for _ in range(N_REGROUPED):
            regrouped.append(pl.BlockSpec((None, classes, tm // classes, WIDTH_A),
                                          lambda bi, i: (bi, 0, i, 0)))
            regrouped_shapes.append(jax.ShapeDtypeStruct((b, classes, s // classes, WIDTH_A), BF16))
    n_tiles = N_REGROUPED * WIDTH_A // LANES
    return pl.pallas_call(
        _proj_kernel,
        grid=(b, s // tm),
        in_specs=[
            pl.BlockSpec((None, tm, d), lambda bi, i: (bi, i, 0)),
            _layer_row(d, layer),
            _mod_spec(d, layer, 1), _mod_spec(d, layer, 0),
            _layer_resident(w_in.shape, layer),
        ],
        out_specs=nat + regrouped,
        out_shape=nat_shapes + regrouped_shapes,
        scratch_shapes=[
            pltpu.VMEM((n_tiles, tm, LANES), F32),
            pltpu.VMEM((n_tiles, tm, LANES), F32),
        ],
        compiler_params=_params(("arbitrary", "arbitrary")),
        name="proj",
    )(x, norm, mod, mod, w_in)


def _stack_heads(qb, is_h0):
    zero = jnp.zeros_like(qb)
    return jnp.concatenate([jnp.where(is_h0, qb, zero), jnp.where(is_h0, zero, qb)], axis=0)


def _unstack(x, is_h0):
    rows = x.shape[0] // 2
    return jnp.where(is_h0, x[:rows], x[rows:])


def _scores(qb, kb, bias, is_h0):
    return lax.dot_general(_stack_heads(qb, is_h0), kb, (((1,), (1,)), ((), ())),
                           preferred_element_type=F32) + bias


def _row_max(s):
    return jnp.broadcast_to(jnp.max(s, axis=-1, keepdims=True), (s.shape[0], LANES))


def _probabilities(s, m):
    return jnp.exp2(s - jnp.tile(m, (1, s.shape[1] // LANES))).astype(BF16)


def _pv_and_sum(p, vb):
    ones = jnp.ones_like(vb)
    both = jnp.dot(p, jnp.concatenate([vb, ones], axis=1), preferred_element_type=F32)
    return both[:, :LANES], both[:, LANES:]


def _three_stage_pipeline(segments):
    finish = None
    for n_groups, stage_qk, stage_softmax, stage_pv, after in segments:
        assert n_groups % 2 == 0 and n_groups >= 2
        stage_qk(0, 0)
        if finish is not None:
            finish[0]()
        stage_qk(1, 1)
        stage_softmax(0, 0)
        if finish is not None:
            finish[1]()

        def two_steps(t, carry, stage_qk=stage_qk, stage_softmax=stage_softmax, stage_pv=stage_pv):
            g = 2 * t + 1
            stage_qk(g + 1, 0)
            stage_softmax(g, 1)
            stage_pv(g - 1, 0)
            stage_qk(g + 2, 1)
            stage_softmax(g + 1, 0)
            stage_pv(g, 1)
            return carry

        lax.fori_loop(0, (n_groups - 2) // 2, two_steps, 0)
        last = n_groups - 1

        def first_trailing(stage_softmax=stage_softmax, stage_pv=stage_pv, last=last):
            stage_softmax(last, 1)
            stage_pv(last - 1, 0)

        def second_trailing(stage_pv=stage_pv, after=after, last=last):
            stage_pv(last, 1)
            if after is not None:
                after()

        finish = (first_trailing, second_trailing)
    finish[0]()
    finish[1]()


def _aligned(start, align):
    return start if isinstance(start, int) else pl.multiple_of(start, align)


def _attn_a_kernel(slopes_ref, q1, k1, v1, q4, k4, v4, q16, k16, v16, o_ref,
                   tab, s_bufs, p_bufs, m_buf, stage, acc_n, max_n, den_n, *, seq):
    hp = pl.program_id(0)
    lane = lax.broadcasted_iota(jnp.int32, (1, LANES), 1)
    is_h0 = lane < HEAD_DIM
    pad = HALF_WINDOW
    n_blocks = seq // A_TQ

    @pl.when(pl.program_id(1) == 0)
    def _build_tables():
        qi = lax.broadcasted_iota(jnp.int32, (A_TQ, A_BAND), 0)
        kj = lax.broadcasted_iota(jnp.int32, (A_TQ, A_BAND), 1)
        for o, (_, dil) in enumerate(DILATED_CONFIGS):
            for hh in range(2):
                slope = slopes_ref[2 * hp + hh]
                head_rows = pl.ds(hh * A_TQ, A_TQ)

                def table(band_lead, keep=None, slope=slope, dil=dil):
                    arel = jnp.abs(kj - band_lead - qi)
                    ok = arel <= HALF_WINDOW
                    if keep is not None:
                        ok = ok & keep
                    return jnp.where(ok, -(slope * (arel * dil).astype(F32)) * LOG2E, NEG)

                tab[o, 0, head_rows, :] = table(pad)
                tab[o, 1, head_rows, :] = table(pad, kj >= pad)
                tab[o, 2, head_rows, :] = table(pad, kj < pad + A_TQ)
                tab[o, 3, head_rows, :] = table(0)
                tab[o, 4, head_rows, :] = table(2 * pad)

    branch_refs = ((q1, k1, v1), (q4, k4, v4), (q16, k16, v16))
    n_sub = seq // SUB_DIL
    segments = []

    for o, ((_, dil), (q_ref, k_ref, v_ref)) in enumerate(zip(DILATED_CONFIGS, branch_refs)):
        n = seq // dil
        nbk = n // A_TQ
        assert n % A_TQ == 0 and nbk >= 2 and n >= A_BAND

        def put(dst, kind, u, jb, row0, value, o=o, dil=dil, nbk=nbk):
            if dil == SUB_DIL:
                dst[o, pl.ds(row0, A_TQ), :] = value
            elif dil == 1:
                sub = A_TQ // SUB_DIL
                stage[kind, u] = value
                for r in range(SUB_DIL):
                    start = _aligned(r * n_sub + jb * sub, sub)
                    dst[o, pl.ds(start, sub), :] = stage[kind, u, pl.ds(r, sub, stride=SUB_DIL), :]
            else:
                c, first = jb // nbk, (jb % nbk) * A_TQ
                start = (c // SUB_DIL) * n_sub + c % SUB_DIL + SUB_DIL * first
                dst[o, pl.ds(start, A_TQ, stride=SUB_DIL), :] = value

        def blocks(g):
            for u in range(A_GROUP):
                jb = g * A_GROUP + u
                row0 = jb * A_TQ
                band0 = jnp.clip(row0 - pad, 0, seq - A_BAND)
                yield u, jb, _aligned(row0, A_TQ), pl.multiple_of(band0, pad)

        def stage_qk(g, slot, o=o, nbk=nbk, q_ref=q_ref, k_ref=k_ref):
            for u, jb, row0, band0 in blocks(g):
                local = jb % nbk
                var = jnp.where(jb == 0, 3, jnp.where(jb == n_blocks - 1, 4, jnp.where(
                    local == 0, 1, jnp.where(local == nbk - 1, 2, 0))))
                s_bufs[slot, u] = _scores(q_ref[pl.ds(row0, A_TQ), :],
                                          k_ref[pl.ds(band0, A_BAND), :], tab[o, var], is_h0)

        def stage_softmax(g, slot, put=put):
            for u in range(A_GROUP):
                m_buf[u] = _row_max(s_bufs[slot, u])
            for u, jb, row0, _ in blocks(g):
                m = m_buf[u]
                p_bufs[slot, u] = _probabilities(s_bufs[slot, u], m)
                put(max_n, 0, u, jb, row0, _unstack(m, is_h0))

        def stage_pv(g, slot, put=put, v_ref=v_ref):
            for u, jb, row0, band0 in blocks(g):
                pv, den = _pv_and_sum(p_bufs[slot, u], v_ref[pl.ds(band0, A_BAND), :])
                put(acc_n, 1, u, jb, row0, _unstack(pv, is_h0))
                put(den_n, 2, u, jb, row0, _unstack(den, is_h0))

        segments.append((seq // (A_TQ * A_GROUP), stage_qk, stage_softmax, stage_pv, None))

    _three_stage_pipeline(segments)

    chunks_per_class = n_sub // COPY_ROWS

    def merge(c, carry):
        rows = pl.ds(pl.multiple_of(c * COPY_ROWS, COPY_ROWS), COPY_ROWS)
        m0, m1, m2 = max_n[0, rows, :], max_n[1, rows, :], max_n[2, rows, :]
        m = jnp.maximum(jnp.maximum(m0, m1), m2)
        w0, w1, w2 = jnp.exp2(m0 - m), jnp.exp2(m1 - m), jnp.exp2(m2 - m)
        num = w0 * acc_n[0, rows, :] + w1 * acc_n[1, rows, :] + w2 * acc_n[2, rows, :]
        den = w0 * den_n[0, rows, :] + w1 * den_n[1, rows, :] + w2 * den_n[2, rows, :]
        r, i0 = c // chunks_per_class, (c % chunks_per_class) * COPY_ROWS
        o_ref[pl.ds(r + SUB_DIL * i0, COPY_ROWS, stride=SUB_DIL), :] = num / den
        return carry

    lax.fori_loop(0, seq // COPY_ROWS, merge, 0)


def _attn_a(qkv, slopes):
    b, s, width = qkv[0].shape
    pairs = width // LANES
    blk = pl.BlockSpec((None, s, LANES), lambda hp, bi, *_: (bi, 0, hp))
    n_branch = len(DILATED_CONFIGS)
    assert s % (2 * A_TQ * A_GROUP) == 0 and s % (SUB_DIL * COPY_ROWS) == 0
    return pl.pallas_call(
        functools.partial(_attn_a_kernel, seq=s),
        grid_spec=pltpu.PrefetchScalarGridSpec(
            num_scalar_prefetch=1,
            grid=(pairs, b),
            in_specs=[blk] * len(qkv),
            out_specs=blk,
            scratch_shapes=[
                pltpu.VMEM((n_branch, 5, 2 * A_TQ, A_BAND), F32),
                pltpu.VMEM((2, A_GROUP, 2 * A_TQ, A_BAND), F32),
                pltpu.VMEM((2, A_GROUP, 2 * A_TQ, A_BAND), BF16),
                pltpu.VMEM((A_GROUP, 2 * A_TQ, LANES), F32),
                pltpu.VMEM((3, A_GROUP, A_TQ, LANES), F32),
                pltpu.VMEM((n_branch, s, LANES), F32),
                pltpu.VMEM((n_branch, s, LANES), F32),
                pltpu.VMEM((n_branch, s, LANES), F32),
            ],
        ),
        out_shape=jax.ShapeDtypeStruct((b, s, width), F32),
        compiler_params=_params(("arbitrary", "arbitrary")),
        name="attn_dilated",
    )(slopes, *qkv)


def _na_variant_rows(n_rows):
    return (NA_ROWS // 2, 0, C_QROWS, n_rows - 2 * C_QROWS, n_rows - C_QROWS)


def _na_bias_tables(rpb, n_rows):
    n_heads, n_dr, n_dc = rpb.shape
    w = GRID_W
    left = w - 1 - (NA_COLS - 1)
    padded = jnp.pad(rpb.astype(F32) * LOG2E, ((0, 0), (0, 0), (left, 2 * w - left - n_dc)),
                     constant_values=NEG)
    skew = jnp.broadcast_to(padded[:, :, None, :], (n_heads, n_dr, w, 2 * w))
    skew = skew.reshape(n_heads, n_dr, 2 * w * w)[:, :, :w * (2 * w - 1)]
    toep = skew.reshape(n_heads, n_dr, w, 2 * w - 1)[:, :, :, w - 1:]
    qc = np.arange(w).reshape(-1, 1)
    kc = np.arange(w).reshape(1, -1)
    cstart = np.clip(qc - NA_COLS // 2, 0, w - NA_COLS)
    col_in = (kc >= cstart) & (kc < cstart + NA_COLS)
    toep = jnp.where(col_in, toep, NEG)
    masked = jnp.full((n_heads, w, w), NEG, F32)
    tables = []
    for r in _na_variant_rows(n_rows):
        base = int(np.clip(r - NA_ROWS // 2, 0, n_rows - C_KROWS))
        q_rows = []
        for a in range(C_QROWS):
            row = r + a
            rstart = int(np.clip(row - NA_ROWS // 2, 0, n_rows - NA_ROWS))
            tiles = []
            for i in range(C_KROWS):
                krow = base + i
                valid = rstart <= krow < rstart + NA_ROWS
                tiles.append(toep[:, krow - row + NA_ROWS - 1] if valid else masked)
            q_rows.append(jnp.concatenate(tiles, axis=-1))
        table = jnp.concatenate(q_rows, axis=-2)
        tables.append(table.reshape(n_heads // 2, 2 * C_QROWS * w, C_KROWS * w))
    return jnp.stack(tables, axis=1)


def _attn_c_kernel(q_ref, k_ref, v_ref, bias_ref, o_ref, s_bufs, p_bufs, m_buf, *, n_rows):
    lane = lax.broadcasted_iota(jnp.int32, (1, LANES), 1)
    is_h0 = lane < HEAD_DIM
    n_blocks = n_rows // C_QROWS
    tq = C_QROWS * GRID_W
    tk = C_KROWS * GRID_W
    assert n_blocks >= 5 and n_blocks % (2 * C_GROUP) == 0

    def blocks(g):
        for u in range(C_GROUP):
            p = g * C_GROUP + u
            base = jnp.clip(p * C_QROWS - NA_ROWS // 2, 0, n_rows - C_KROWS)
            k0 = base * GRID_W
            yield u, p, _aligned(p * tq, tq), pl.multiple_of(k0, GRID_W)

    def stage_qk(g, slot):
        for u, p, q0, k0 in blocks(g):
            var = jnp.where(p == 0, 1, jnp.where(p == 1, 2, jnp.where(
                p == n_blocks - 2, 3, jnp.where(p == n_blocks - 1, 4, 0))))
            s_bufs[slot, u] = _scores(q_ref[pl.ds(q0, tq), :], k_ref[pl.ds(k0, tk), :],
                                      bias_ref[var], is_h0)

    def stage_softmax(g, slot):
        for u in range(C_GROUP):
            m_buf[u] = _row_max(s_bufs[slot, u])
        for u in range(C_GROUP):
            p_bufs[slot, u] = _probabilities(s_bufs[slot, u], m_buf[u])

    def stage_pv(g, slot):
        for u, _, q0, k0 in blocks(g):
            pv, den = _pv_and_sum(p_bufs[slot, u], v_ref[pl.ds(k0, tk), :])
            o_ref[pl.ds(q0, tq), :] = _unstack(pv, is_h0) / _unstack(den, is_h0)

    _three_stage_pipeline([(n_blocks // C_GROUP, stage_qk, stage_softmax, stage_pv, None)])


def _attn_c(qc, kc, vc, bias):
    b, s, width = qc.shape
    pairs = width // LANES
    tq = C_QROWS * GRID_W
    tk = C_KROWS * GRID_W
    blk = pl.BlockSpec((None, s, LANES), lambda bi, hp: (bi, 0, hp))
    return pl.pallas_call(
        functools.partial(_attn_c_kernel, n_rows=s // GRID_W),
        grid=(b, pairs),
        in_specs=[blk, blk, blk,
                  pl.BlockSpec((None,) + bias.shape[1:], lambda bi, hp: (hp, 0, 0, 0))],
        out_specs=blk,
        out_shape=jax.ShapeDtypeStruct((b, s, width), F32),
        scratch_shapes=[
            pltpu.VMEM((2, C_GROUP, 2 * tq, tk), F32),
            pltpu.VMEM((2, C_GROUP, 2 * tq, tk), BF16),
            pltpu.VMEM((C_GROUP, 2 * tq, LANES), F32),
        ],
        compiler_params=_params(("arbitrary", "arbitrary")),
        name="attn_neighbourhood",
    )(qc, kc, vc, bias)


def _mix_ffn_kernel(oa_ref, oc_ref, u_ref, up_ref, un_ref, x_ref,
                    na_ref, nc_ref, wpool_ref, ps_ref, wout_ref, g1_ref,
                    nf_ref, sc2_ref, sh2_ref, w1_ref, w2_ref, g2_ref, nfin_ref,
                    o_ref, cat, act, ext, *, tm, seq, d_ff, final_norm):
    i = pl.program_id(1)
    halo = POOL_HALO

    width_b = u_ref.shape[-1]
    ext[0:halo, :] = jnp.where(i > 0, up_ref[...], 0.0)
    ext[halo:halo + tm, :] = u_ref[...]
    ext[halo + tm:halo + tm + halo, :] = jnp.where(i < pl.num_programs(1) - 1, un_ref[...], 0.0)
    lane = lax.broadcasted_iota(jnp.int32, (1, width_b), 1)
    wa = oa_ref.shape[-1]

    part = tm // FFN_PARTS
    starts = list(range(0, tm, part))
    mixed = {}

    def mixer(r0):
        rows = pl.ds(r0, part)
        t = i * tm + r0 + lax.broadcasted_iota(jnp.int32, (part, 1), 0)
        num = jnp.zeros((part, width_b), F32)
        den = jnp.zeros((part, width_b), F32)
        acc = jnp.zeros((part, width_b), F32)
        done = 0
        for g, w in enumerate(POOL_WINDOWS):
            for d in list(range(-(w // 2), -done)) + list(range(done, w // 2)):
                acc = acc + ext[halo + r0 + d:halo + r0 + d + part, :]
            done = w // 2
            lo = jnp.clip(t - w // 2, 0, seq - 1)
            hi = jnp.clip(t + w // 2 - 1, 0, seq - 1)
            cnt = (hi - lo + 1).astype(F32)
            in_group = (lane >= g * POOL_GROUP_DIM) & (lane < (g + 1) * POOL_GROUP_DIM)
            num = jnp.where(in_group, acc, num)
            den = jnp.where(in_group, cnt, den)
            yield
        pooled = num / den - u_ref[rows, :]
        y = jnp.dot(pooled.astype(BF16), wpool_ref[...], preferred_element_type=F32) * ps_ref[...]
        cat[rows, wa:wa + width_b] = y.astype(BF16)
        yield
        cat[rows, 0:wa] = _rms(oa_ref[rows, :], na_ref[...]).astype(BF16)
        yield
        cat[rows, wa + width_b:] = _rms(oc_ref[rows, :], nc_ref[...]).astype(BF16)
        yield
        mix = jnp.dot(cat[rows, :], wout_ref[...], preferred_element_type=F32)
        x1 = x_ref[rows, :] + g1_ref[...] * mix
        yield
        h2 = (_rms(x1, nf_ref[...]) * (1.0 + sc2_ref[...]) + sh2_ref[...]).astype(BF16)
        mixed[r0] = (x1, h2)

    def swiglu(r0, background):
        rows = pl.ds(r0, part)
        x1, h2 = mixed.pop(r0)
        for c in range(d_ff // FFN_CHUNK):
            lo = c * FFN_CHUNK
            gate = jnp.dot(h2, w1_ref[:, lo:lo + FFN_CHUNK], preferred_element_type=F32)
            up = jnp.dot(h2, w1_ref[:, d_ff + lo:d_ff + lo + FFN_CHUNK],
                         preferred_element_type=F32)
            act[rows, lo:lo + FFN_CHUNK] = (gate * jax.nn.sigmoid(gate) * up).astype(BF16)
            next(background, None)
        for _ in background:
            pass
        ffn = jnp.dot(act[rows, :], w2_ref[...], preferred_element_type=F32)
        x2 = x1 + g2_ref[...] * ffn
        if final_norm:
            x2 = _rms(x2, nfin_ref[...])
        o_ref[rows, :] = x2

    pieces = {r0: mixer(r0) for r0 in starts}
    for _ in pieces[starts[0]]:
        pass
    for k, r0 in enumerate(starts):
        swiglu(r0, pieces[starts[k + 1]] if k + 1 < len(starts) else iter(()))


def _mix_ffn(oa, oc, ub, x, norm_a, norm_c, w_pool_bd, pool_scale, w_out, mod,
             norm_ffn, w1, w2, norm_final, layer, final_norm):
    b, s, d = x.shape
    tm = FFN_TM
    d_ff = w2.shape[1]
    width_a, width_c, width_b = oa.shape[-1], oc.shape[-1], ub.shape[-1]
    assert d_ff % FFN_CHUNK == 0 and tm % POOL_HALO == 0
    hb = tm // POOL_HALO
    n_hb = s // POOL_HALO
    tile = lambda width: pl.BlockSpec((None, tm, width), lambda bi, i: (bi, i, 0))
    return pl.pallas_call(
        functools.partial(_mix_ffn_kernel, tm=tm, seq=s, d_ff=d_ff, final_norm=final_norm),
        grid=(b, s // tm),
        in_specs=[
            tile(width_a), tile(width_c), tile(width_b),
            pl.BlockSpec((None, POOL_HALO, width_b),
                         lambda bi, i: (bi, jnp.maximum(i * hb - 1, 0), 0)),
            pl.BlockSpec((None, POOL_HALO, width_b),
                         lambda bi, i: (bi, jnp.minimum((i + 1) * hb, n_hb - 1), 0)),
            tile(d),
            _layer_row(width_a, layer), _layer_row(width_c, layer),
            _layer_resident(w_pool_bd.shape, layer), _layer_row(width_b, layer),
            _layer_resident(w_out.shape, layer), _mod_spec(d, layer, 2),
            _layer_row(d, layer), _mod_spec(d, layer, 4), _mod_spec(d, layer, 3),
            _layer_resident(w1.shape, layer), _layer_resident(w2.shape, layer),
            _mod_spec(d, layer, 5),
            pl.BlockSpec((1, d), lambda bi, i: (0, 0)),
        ],
        out_specs=tile(d),
        out_shape=jax.ShapeDtypeStruct((b, s, d), F32),
        scratch_shapes=[
            pltpu.VMEM((tm, d), BF16),
            pltpu.VMEM((tm, d_ff), BF16),
            pltpu.VMEM((tm + 2 * POOL_HALO, width_b), F32),
        ],
        compiler_params=_params(("arbitrary", "arbitrary")),
        name="mix_ffn",
    )(oa, oc, ub, ub, ub, x, norm_a, norm_c, w_pool_bd, pool_scale, w_out, mod,
      norm_ffn, mod, mod, w1, w2, mod, norm_final.reshape(1, d))


def _block_diag(w_pool):
    depth, g, gd, _ = w_pool.shape
    out = jnp.zeros((depth, g * gd, g * gd), w_pool.dtype)
    for j in range(g):
        out = out.at[:, j * gd:(j + 1) * gd, j * gd:(j + 1) * gd].set(w_pool[:, j])
    return out


def kernel(x, c, w_ada, b_ada, norm_mix, w_in, norm_a_out, norm_c_out, w_pool, pool_scale, rpb,
           w_out, norm_ffn, w_ffn_in, w_ffn_out, norm_final):
    depth = w_ada.shape[0]
    b, s, d = x.shape
    assert s % GRID_W == 0 and s % PROJ_TM == 0 and s % FFN_TM == 0

    mod = _ada(c, w_ada, b_ada).reshape(depth, b, N_MOD, 1, d)
    heads = jnp.arange(N_HEADS_A, dtype=F32)
    slopes = 2.0 ** (-8.0 * (heads + 1.0) / N_HEADS_A)

    per_channel = lambda p: p.reshape(depth, 1, p.shape[-1])
    w_in_b, w_out_b = w_in.astype(BF16), w_out.astype(BF16)
    w1_b, w2_b = w_ffn_in.astype(BF16), w_ffn_out.astype(BF16)
    w_pool_b = _block_diag(w_pool).astype(BF16)
    norm_mix_r, norm_a_r, norm_c_r = per_channel(norm_mix), per_channel(norm_a_out), per_channel(norm_c_out)
    pool_scale_r, norm_ffn_r = per_channel(pool_scale), per_channel(norm_ffn)

    for l in range(depth):
        proj = _proj(x, norm_mix_r, mod, w_in_b, l)
        ub, qc, kc, vc = proj[N_REGROUPED:len(_PROJ_OUTS)]
        qkv = proj[:N_REGROUPED] + proj[len(_PROJ_OUTS):]
        oa = _attn_a([t.reshape(b, s, WIDTH_A) for t in qkv], slopes)
        oc = _attn_c(qc, kc, vc, _na_bias_tables(rpb[l], s // GRID_W))
        x = _mix_ffn(oa, oc, ub, x, norm_a_r, norm_c_r, w_pool_b, pool_scale_r, w_out_b, mod,
                     norm_ffn_r, w1_b, w2_b, norm_final, l, final_norm=(l == depth - 1))
    return x
```
